```python
import jax, jax.numpy as jnp
from jax import lax
import numpy as np

D_MODEL = 4096
BATCH = 4
SEQ = 2048
DEPTH = 2
DEC_BATCH = 8
DEC_SEQ = 1
PAST_LEN = 16384
PAGE_SIZE = 128

HEAD_DIM = 128
DILATION_GROUPS = ((128, 1), (512, 4), (2048, 16))
N_GROUPS = len(DILATION_GROUPS)
HEADS_PER_GROUP = D_MODEL // 512
N_ATTN_HEADS = N_GROUPS * HEADS_PER_GROUP
ATTN_WIDTH = N_ATTN_HEADS * HEAD_DIM
ATTN_OUT = HEADS_PER_GROUP * HEAD_DIM
Q_BLOCK = 128
CHUNK = 128
GM_WIDTH = D_MODEL // 2
GM_GROUPS = 16
GM_GROUP_DIM = GM_WIDTH // GM_GROUPS
D_FF = -(-8 * D_MODEL // (3 * 256)) * 256
PLE_DIM = 256
NUM_BUCKETS = 32
REL_MAX_DIST = 2048
IN_COLS = 3 * ATTN_WIDTH + 2 * GM_WIDTH + 2 * D_MODEL
NORM_EPS = 1e-6
NEG_INF = -1e30

kernel_name = "hybrid_dilated_attn_gmlp_decoder_step"


def _rmsnorm(x, g):
    xf = x.astype(jnp.float32)
    y = xf * lax.rsqrt(jnp.mean(xf * xf, axis=-1, keepdims=True) + NORM_EPS)
    return (y * g.astype(jnp.float32)).astype(x.dtype)


def _t5_bucket(dist):
    dist = np.asarray(dist)
    max_exact = NUM_BUCKETS // 2
    large = max_exact + (np.log(np.maximum(dist, 1) / max_exact) / np.log(REL_MAX_DIST / max_exact)
                         * (NUM_BUCKETS - max_exact)).astype(np.int32)
    large = np.minimum(large, NUM_BUCKETS - 1)
    return np.where(dist < max_exact, dist, large).astype(np.int32)


def _group_bias(rel_bias, gi):
    window, dil = DILATION_GROUPS[gi]
    idx = _t5_bucket(np.arange(window // dil + 1) * dil)
    hs = slice(gi * HEADS_PER_GROUP, (gi + 1) * HEADS_PER_GROUP)
    return jnp.take(rel_bias, idx, axis=0)[:, hs].astype(jnp.float32)


def _dilated_attention(q, k, v, window, dil, bias):
    B, Tq, H, HD = q.shape
    Tk = k.shape[1]
    n_k = window // dil + 1
    offs = jnp.arange(n_k) * dil
    qb = min(Q_BLOCK, Tq)
    nb = -(-Tq // qb)
    qp = jnp.pad(q, ((0, 0), (0, nb * qb - Tq), (0, 0), (0, 0)))
    bias_t = bias.T[None, None]

    def one_block(bi):
        qblk = lax.dynamic_slice_in_dim(qp, bi * qb, qb, axis=1).astype(jnp.float32)
        qi = bi * qb + jnp.arange(qb) + (Tk - Tq)
        idx = qi[:, None] - offs[None, :]
        valid = (idx >= 0) & (idx < Tk)
        idx = jnp.clip(idx, 0, Tk - 1).reshape(-1)
        kg = jnp.take(k, idx, axis=1).reshape(B, qb, n_k, H, HD).astype(jnp.float32)
        vg = jnp.take(v, idx, axis=1).reshape(B, qb, n_k, H, HD).astype(jnp.float32)
        s = jnp.einsum('bqhd,bqjhd->bqhj', qblk, kg) + bias_t
        s = jnp.where(valid[None, :, None, :], s, NEG_INF)
        m = jnp.max(s, axis=-1, keepdims=True)
        e = jnp.exp(s - m)
        den = jnp.sum(e, axis=-1)
        o = jnp.einsum('bqhj,bqjhd->bqhd', e, vg) / den[..., None]
        return o, m[..., 0] + jnp.log(den)

    o, lse = lax.map(one_block, jnp.arange(nb))
    o = jnp.transpose(o, (1, 0, 2, 3, 4)).reshape(B, nb * qb, H, HD)[:, :Tq]
    lse = jnp.transpose(lse, (1, 0, 2, 3)).reshape(B, nb * qb, H)[:, :Tq]
    return o, lse


def _spatial_gating(u, v, w_sgu, b_sgu):
    B, T, C = v.shape
    nc = -(-T // CHUNK)
    vp = jnp.pad(v, ((0, 0), (0, nc * CHUNK - T), (0, 0))).reshape(B, nc, CHUNK, GM_GROUPS, GM_GROUP_DIM)
    w_c = w_sgu * jnp.tril(jnp.ones((CHUNK, CHUNK), w_sgu.dtype))[None]
    mix = jnp.einsum('gts,bnsgc->bntgc', w_c, vp) + b_sgu.T[None, None, :, :, None]
    mix = mix.reshape(B, nc * CHUNK, C)[:, :T]
    return u * mix


def _layer(x, p, kv_cache, biases, g_mix, w_in, g_q, g_k, g_sgu, w_sgu, b_sgu, w_attn_br, w_sgu_br,
           w_out, g_ffn, w_ffn_gate, w_ffn_up, w_ffn_down, g_ple, w_ple_gate, w_ple_proj):
    B, T, _ = x.shape
    h = _rmsnorm(x, g_mix)
    z = h @ w_in
    cuts = np.cumsum([ATTN_WIDTH, ATTN_WIDTH, ATTN_WIDTH, GM_WIDTH, GM_WIDTH, D_MODEL]).tolist()
    q, k, v, u_pre, v_pre, gate_a, gate_b = jnp.split(z, cuts, axis=-1)
    q = _rmsnorm(q.reshape(B, T, N_ATTN_HEADS, HEAD_DIM), g_q) * (HEAD_DIM ** -0.5)
    k = _rmsnorm(k.reshape(B, T, N_ATTN_HEADS, HEAD_DIM), g_k)
    v = v.reshape(B, T, N_ATTN_HEADS, HEAD_DIM)

    outs, lses, new_kv = [], [], []
    for gi, (window, dil) in enumerate(DILATION_GROUPS):
        hs = slice(gi * HEADS_PER_GROUP, (gi + 1) * HEADS_PER_GROUP)
        kv_new = jnp.stack([k[:, :, hs], v[:, :, hs]], axis=2)
        if kv_cache is None:
            kv_all = kv_new
            keep = min(window, T)
        else:
            kv_all = jnp.concatenate([kv_cache[gi].astype(kv_new.dtype), kv_new], axis=1)
            keep = kv_cache[gi].shape[1]
        new_kv.append(kv_all[:, kv_all.shape[1] - keep:])
        o, lse = _dilated_attention(q[:, :, hs], kv_all[:, :, 0], kv_all[:, :, 1], window, dil, biases[gi])
        outs.append(o)
        lses.append(lse)
    wts = jax.nn.softmax(jnp.stack(lses, axis=0), axis=0)
    attn = sum(wts[gi][..., None] * outs[gi] for gi in range(N_GROUPS))
    a_br = attn.reshape(B, T, ATTN_OUT).astype(x.dtype) @ w_attn_br

    u = jax.nn.gelu(u_pre)
    vs = _rmsnorm(jax.nn.gelu(v_pre), g_sgu)
    b_br = _spatial_gating(u, vs, w_sgu, b_sgu) @ w_sgu_br

    merged = jax.nn.sigmoid(gate_a) * a_br + jax.nn.sigmoid(gate_b) * b_br
    x = x + merged @ w_out

    h2 = _rmsnorm(x, g_ffn)
    x = x + (jax.nn.silu(h2 @ w_ffn_gate) * (h2 @ w_ffn_up)) @ w_ffn_down

    x = x + jax.nn.sigmoid(_rmsnorm(x, g_ple) @ w_ple_gate) * (p @ w_ple_proj)
    return x, new_kv, vs


def _trunk(x, p, caches, rel_bias, g_mix, w_in, g_q, g_k, g_sgu, w_sgu, b_sgu, w_attn_br, w_sgu_br,
           w_out, g_ffn, w_ffn_gate, w_ffn_up, w_ffn_down, g_ple, w_ple_gate, w_ple_proj):
    biases = [_group_bias(rel_bias, gi) for gi in range(N_GROUPS)]
    kv_layers = [[] for _ in range(N_GROUPS)]
    v_layers = []
    for i in range(DEPTH):
        layer_cache = None if caches is None else [c[i] for c in caches]
        x, kv_new, v_rows = _layer(x, p[i], layer_cache, biases, g_mix[i], w_in[i], g_q[i], g_k[i],
                                   g_sgu[i], w_sgu[i], b_sgu[i], w_attn_br[i], w_sgu_br[i], w_out[i],
                                   g_ffn[i], w_ffn_gate[i], w_ffn_up[i], w_ffn_down[i], g_ple[i],
                                   w_ple_gate[i], w_ple_proj[i])
        for gi in range(N_GROUPS):
            kv_layers[gi].append(kv_new[gi])
        v_layers.append(v_rows)
    return x, [jnp.stack(s, axis=0) for s in kv_layers], jnp.stack(v_layers, axis=0)


def setup_inputs(seed: int = 0) -> dict:
    key = jax.random.key(seed)
    ks = jax.random.split(key, 32)
    f32 = jnp.float32

    def nrm(k, shape, scale):
        return jax.random.normal(k, shape, f32) * scale

    def gain(k, shape):
        return 1.0 + 0.05 * jax.random.normal(k, shape, f32)

    lens = [min(w, PAST_LEN) for w, _ in DILATION_GROUPS]
    return {
        "x_prompt": nrm(ks[0], (BATCH, SEQ, D_MODEL), 1.0),
        "x_sample": nrm(ks[1], (DEC_BATCH, DEC_SEQ, D_MODEL), 1.0),
        "cache_kv_w128": nrm(ks[2], (DEPTH, DEC_BATCH, lens[0], 2, HEADS_PER_GROUP, HEAD_DIM), 1.0),
        "cache_kv_w512": nrm(ks[3], (DEPTH, DEC_BATCH, lens[1], 2, HEADS_PER_GROUP, HEAD_DIM), 1.0),
        "cache_kv_w2048": nrm(ks[4], (DEPTH, DEC_BATCH, lens[2], 2, HEADS_PER_GROUP, HEAD_DIM), 1.0),
        "p_prompt": nrm(ks[5], (DEPTH, BATCH, SEQ, PLE_DIM), 1.0),
        "p_sample": nrm(ks[6], (DEPTH, DEC_BATCH, DEC_SEQ, PLE_DIM), 1.0),
        "rel_bias": nrm(ks[7], (NUM_BUCKETS, N_ATTN_HEADS), 0.5),
        "g_mix": gain(ks[8], (DEPTH, D_MODEL)),
        "w_in": nrm(ks[9], (DEPTH, D_MODEL, IN_COLS), D_MODEL ** -0.5),
        "g_q": gain(ks[10], (DEPTH, HEAD_DIM)),
        "g_k": gain(ks[11], (DEPTH, HEAD_DIM)),
        "g_sgu": gain(ks[12], (DEPTH, GM_WIDTH)),
        "w_sgu": nrm(ks[13], (DEPTH, GM_GROUPS, CHUNK, CHUNK), CHUNK ** -0.5),
        "b_sgu": gain(ks[14], (DEPTH, GM_GROUPS, CHUNK)),
        "w_attn_br": nrm(ks[15], (DEPTH, ATTN_OUT, D_MODEL), ATTN_OUT ** -0.5),
        "w_sgu_br": nrm(ks[16], (DEPTH, GM_WIDTH, D_MODEL), GM_WIDTH ** -0.5),
        "w_out": nrm(ks[17], (DEPTH, D_MODEL, D_MODEL), D_MODEL ** -0.5),
        "g_ffn": gain(ks[18], (DEPTH, D_MODEL)),
        "w_ffn_gate": nrm(ks[19], (DEPTH, D_MODEL, D_FF), D_MODEL ** -0.5),
        "w_ffn_up": nrm(ks[20], (DEPTH, D_MODEL, D_FF), D_MODEL ** -0.5),
        "w_ffn_down": nrm(ks[21], (DEPTH, D_FF, D_MODEL), D_FF ** -0.5),
        "g_ple": gain(ks[22], (DEPTH, D_MODEL)),
        "w_ple_gate": nrm(ks[23], (DEPTH, D_MODEL, D_MODEL), D_MODEL ** -0.5),
        "w_ple_proj": nrm(ks[24], (DEPTH, PLE_DIM, D_MODEL), PLE_DIM ** -0.5),
    }


def reference(x_prompt, x_sample, cache_kv_w128, cache_kv_w512, cache_kv_w2048, p_prompt, p_sample,
              rel_bias, g_mix, w_in, g_q, g_k, g_sgu, w_sgu, b_sgu, w_attn_br, w_sgu_br, w_out, g_ffn,
              w_ffn_gate, w_ffn_up, w_ffn_down, g_ple, w_ple_gate, w_ple_proj):
    y_prompt, kv_p, _ = _trunk(x_prompt, p_prompt, None, rel_bias, g_mix, w_in, g_q, g_k, g_sgu, w_sgu,
                               b_sgu, w_attn_br, w_sgu_br, w_out, g_ffn, w_ffn_gate, w_ffn_up,
                               w_ffn_down, g_ple, w_ple_gate, w_ple_proj)
    y_sample, kv_s, sgu_v_sample = _trunk(x_sample, p_sample, (cache_kv_w128, cache_kv_w512, cache_kv_w2048),
                                          rel_bias, g_mix, w_in, g_q, g_k, g_sgu, w_sgu, b_sgu, w_attn_br,
                                          w_sgu_br, w_out, g_ffn, w_ffn_gate, w_ffn_up, w_ffn_down, g_ple,
                                          w_ple_gate, w_ple_proj)
    kv_w128_prompt, kv_w512_prompt, kv_w2048_prompt = kv_p
    kv_w128_sample, kv_w512_sample, kv_w2048_sample = kv_s
    return (y_prompt, y_sample, kv_w128_prompt, kv_w512_prompt, kv_w2048_prompt,
            kv_w128_sample, kv_w512_sample, kv_w2048_sample, sgu_v_sample)
```

```python
import functools

import numpy as np
import jax
import jax.numpy as jnp
from jax import lax
from jax.experimental import pallas as pl
from jax.experimental.pallas import tpu as pltpu

D_MODEL = 4096
HEAD_DIM = 128
DILATION_GROUPS = ((128, 1), (512, 4), (2048, 16))
N_GROUPS = 3
HEADS_PER_GROUP = 8
N_ATTN_HEADS = 24
ATTN_WIDTH = 3072
ATTN_OUT = 1024
BLK = 128
GM_WIDTH = 2048
GM_GROUPS = 16
D_FF = 11008
D_FF_PAD = 11264
PLE_DIM = 256
NUM_BUCKETS = 32
REL_MAX_DIST = 2048
NORM_EPS = 1e-6
NEG_INF = -1e30
Q_SCALE = HEAD_DIM ** -0.5

TN = 1024
COL_Q, COL_K, COL_V, COL_U, COL_GATE = 0, 3, 6, 9, 13
N_IN_TILES = 21

V7X_VMEM_REQUEST_CAP = 60000 * 1024
COMPILER_TEMP_BYTES = 10 * 1024 * 1024

F32 = jnp.float32
BF16 = jnp.bfloat16


def _params(sems, window_bytes):
    limit = min(V7X_VMEM_REQUEST_CAP, 2 * window_bytes + COMPILER_TEMP_BYTES)
    return pltpu.CompilerParams(dimension_semantics=sems, vmem_limit_bytes=int(limit))


def _nbytes(shape, dtype):
    return int(np.prod(shape)) * jnp.dtype(dtype).itemsize


def _gelu(x):
    return x * (0.5 * (1.0 + jnp.tanh(0.7978845608028654 * (x + 0.044715 * (x * x * x)))))


def _silu(x):
    return x * jax.nn.sigmoid(x)


_ACT = {"gelu": _gelu, "sigmoid": jax.nn.sigmoid, "none": lambda x: x}


def _rms(x):
    return x * lax.rsqrt(jnp.mean(x * x, axis=-1, keepdims=True) + NORM_EPS)


def _dot(a, b):
    return jnp.dot(a.astype(BF16), b.astype(BF16), preferred_element_type=F32)


def _dot_nt(a, b):
    return lax.dot_general(a, b, (((1,), (1,)), ((), ())), preferred_element_type=F32)


def _rmsnorm_kernel(x_ref, g_ref, o_ref):
    o_ref[...] = (_rms(x_ref[...]) * g_ref[...]).astype(o_ref.dtype)


def _rmsnorm(x, g, out_dtype, tm):
    m, d = x.shape
    return pl.pallas_call(
        _rmsnorm_kernel,
        grid=(m // tm,),
        in_specs=[pl.BlockSpec((tm, d), lambda i: (i, 0)), pl.BlockSpec((1, d), lambda i: (0, 0))],
        out_specs=pl.BlockSpec((tm, d), lambda i: (i, 0)),
        out_shape=jax.ShapeDtypeStruct((m, d), out_dtype),
        compiler_params=_params(("parallel",), _nbytes((tm, d), F32) + _nbytes((tm, d), out_dtype)),
        name="rmsnorm",
    )(x, g.reshape(1, d))


def _proj_act_kernel(x_ref, w_ref, o_ref, *, act):
    o_ref[...] = _ACT[act](_dot(x_ref[...], w_ref[...])).astype(o_ref.dtype)


def _proj_act(h, w, col0, n_tiles, act, out_dtype, tm):
    m, k = h.shape
    win = _nbytes((tm, k), h.dtype) + _nbytes((k, TN), BF16) + _nbytes((tm, TN), out_dtype)
    return pl.pallas_call(
        functools.partial(_proj_act_kernel, act=act),
        grid=(m // tm, n_tiles),
        in_specs=[pl.BlockSpec((tm, k), lambda i, j: (i, 0)),
                  pl.BlockSpec((k, TN), lambda i, j: (0, j + col0))],
        out_specs=pl.BlockSpec((tm, TN), lambda i, j: (i, j)),
        out_shape=jax.ShapeDtypeStruct((m, n_tiles * TN), out_dtype),
        compiler_params=_params(("parallel", "arbitrary"), win),
        name="proj_" + act,
    )(h, w)


def _q_kernel(x_ref, w_ref, g_ref, o_ref):
    acc = _dot(x_ref[...], w_ref[...])
    for hh in range(HEADS_PER_GROUP):
        a = acc[:, hh * HEAD_DIM:(hh + 1) * HEAD_DIM]
        o_ref[0, hh] = ((_rms(a) * g_ref[...]) * Q_SCALE).astype(o_ref.dtype)


def _q_proj(h, w, g_q, batch, seq, tm):
    m, k = h.shape
    tpb = seq // tm
    win = _nbytes((tm, k), BF16) + _nbytes((k, TN), BF16) + _nbytes((tm, TN), BF16)
    return pl.pallas_call(
        _q_kernel,
        grid=(m // tm, N_GROUPS),
        in_specs=[pl.BlockSpec((tm, k), lambda i, j: (i, 0)),
                  pl.BlockSpec((k, TN), lambda i, j: (0, j + COL_Q)),
                  pl.BlockSpec((1, HEAD_DIM), lambda i, j: (0, 0))],
        out_specs=pl.BlockSpec((1, HEADS_PER_GROUP, tm, HEAD_DIM),
                               lambda i, j: (i // tpb, j, i % tpb, 0)),
        out_shape=jax.ShapeDtypeStruct((batch, N_ATTN_HEADS, seq, HEAD_DIM), BF16),
        compiler_params=_params(("parallel", "arbitrary"), win),
        name="q_proj",
    )(h, w, g_q.reshape(1, HEAD_DIM))


def _kv_kernel(x_ref, wk_ref, wv_ref, g_ref, khm_ref, vhm_ref, kv_ref, *, rows):
    x = x_ref[...]
    k = _dot(x, wk_ref[...])
    v = _dot(x, wv_ref[...])
    tm = x.shape[0]
    for hh in range(HEADS_PER_GROUP):
        sl = slice(hh * HEAD_DIM, (hh + 1) * HEAD_DIM)
        kn = _rms(k[:, sl]) * g_ref[...]
        vh = v[:, sl]
        khm_ref[0, hh] = kn.astype(BF16)
        vhm_ref[0, hh] = vh.astype(BF16)
        kv_ref[0, :, sl] = kn[tm - rows:]
        kv_ref[0, :, ATTN_OUT + hh * HEAD_DIM:ATTN_OUT + (hh + 1) * HEAD_DIM] = vh[tm - rows:]


def _kv_proj(h, w, g_k, gi, batch, seq, tm):
    m, k = h.shape
    tpb = seq // tm
    keep = min(DILATION_GROUPS[gi][0], seq)
    rows = min(keep, tm)
    kv_tiles = keep // rows
    kv_map = lambda i: (i // tpb, jnp.maximum(i % tpb - (tpb - kv_tiles), 0), 0)
    hm_spec = pl.BlockSpec((1, HEADS_PER_GROUP, tm, HEAD_DIM), lambda i: (i // tpb, 0, i % tpb, 0))
    hm_shape = jax.ShapeDtypeStruct((batch, HEADS_PER_GROUP, seq, HEAD_DIM), BF16)
    win = (_nbytes((tm, k), BF16) + 2 * _nbytes((k, TN), BF16) + 2 * _nbytes((tm, TN), BF16)
           + _nbytes((rows, 2 * ATTN_OUT), F32))
    return pl.pallas_call(
        functools.partial(_kv_kernel, rows=rows),
        grid=(m // tm,),
        in_specs=[pl.BlockSpec((tm, k), lambda i: (i, 0)),
                  pl.BlockSpec((k, TN), lambda i: (0, COL_K + gi), pipeline_mode=pl.Buffered(1)),
                  pl.BlockSpec((k, TN), lambda i: (0, COL_V + gi), pipeline_mode=pl.Buffered(1)),
                  pl.BlockSpec((1, HEAD_DIM), lambda i: (0, 0))],
        out_specs=[hm_spec, hm_spec, pl.BlockSpec((1, rows, 2 * ATTN_OUT), kv_map)],
        out_shape=[hm_shape, hm_shape, jax.ShapeDtypeStruct((batch, keep, 2 * ATTN_OUT), F32)],
        compiler_params=_params(("arbitrary",), win),
        name="kv_proj",
    )(h, w, w, g_k.reshape(1, HEAD_DIM))


def _attn_block(qb, kc, vc, bias_c, kp=None, vp=None, bias_p=None):
    s_c = _dot_nt(qb, kc) + bias_c
    m = jnp.max(s_c, axis=-1, keepdims=True)
    if kp is not None:
        s_p = _dot_nt(qb, kp) + bias_p
        m = jnp.maximum(m, jnp.max(s_p, axis=-1, keepdims=True))
    e_c = jnp.exp(s_c - m)
    den = jnp.sum(e_c, axis=-1, keepdims=True)
    num = _dot(e_c, vc)
    if kp is not None:
        e_p = jnp.exp(s_p - m)
        den = den + jnp.sum(e_p, axis=-1, keepdims=True)
        num = num + _dot(e_p, vp)
    o = num / den
    lse = m + jnp.log(den)
    return o, jnp.broadcast_to(lse, o.shape)


def _attn_kernel(q0, q1, q2, k0, k1, k2, v0, v1, v2, bias_ref, o_ref,
                 o0_sc, o1_sc, o2_sc, l0_sc, l1_sc, l2_sc, *, seq):
    def run_group(gi, q_ref, k_ref, v_ref, o_sc, l_sc):
        dil = DILATION_GROUPS[gi][1]
        nblk = seq // dil // BLK
        for r in range(dil):
            lanes = slice(r * HEAD_DIM, (r + 1) * HEAD_DIM)

            def body(blk, carry):
                r0 = pl.multiple_of(blk * BLK, BLK)
                p0 = pl.multiple_of(jnp.maximum(blk - 1, 0) * BLK, BLK)
                bias_c = bias_ref[gi, 0, :, BLK:2 * BLK]
                bias_p = bias_ref[gi, 0, :, 0:BLK] + jnp.where(blk > 0, 0.0, NEG_INF)
                o, lse = _attn_block(
                    q_ref[0, 0, pl.ds(r0, BLK), lanes],
                    k_ref[0, 0, pl.ds(r0, BLK), lanes], v_ref[0, 0, pl.ds(r0, BLK), lanes], bias_c,
                    k_ref[0, 0, pl.ds(p0, BLK), lanes], v_ref[0, 0, pl.ds(p0, BLK), lanes], bias_p)
                if dil == 1:
                    rows = pl.ds(r0, BLK)
                else:
                    rows = pl.ds(blk * (BLK * dil) + r, BLK, stride=dil)
                o_sc[rows, :] = o
                l_sc[rows, :] = lse
                return carry

            if nblk == 1:
                o, lse = _attn_block(q_ref[0, 0, :, lanes], k_ref[0, 0, :, lanes],
                                     v_ref[0, 0, :, lanes], bias_ref[gi, 0, :, BLK:2 * BLK])
                o_sc[pl.ds(r, BLK, stride=dil), :] = o
                l_sc[pl.ds(r, BLK, stride=dil), :] = lse
            else:
                lax.fori_loop(0, nblk, body, 0)

    run_group(0, q0, k0, v0, o0_sc, l0_sc)
    run_group(1, q1, k1, v1, o1_sc, l1_sc)
    run_group(2, q2, k2, v2, o2_sc, l2_sc)

    def combine(c, carry):
        rows = pl.ds(pl.multiple_of(c * BLK, BLK), BLK)
        l0, l1, l2 = l0_sc[rows, :], l1_sc[rows, :], l2_sc[rows, :]
        top = jnp.maximum(jnp.maximum(l0, l1), l2)
        w0, w1, w2 = jnp.exp(l0 - top), jnp.exp(l1 - top), jnp.exp(l2 - top)
        mixed = (w0 * o0_sc[rows, :] + w1 * o1_sc[rows, :] + w2 * o2_sc[rows, :]) / (w0 + w1 + w2)
        o_ref[0, rows, :] = mixed.astype(o_ref.dtype)
        return carry

    lax.fori_loop(0, seq // BLK, combine, 0)


def _attention(q_hm, ks, vs, bias_tab, batch, seq):
    ins, specs = [], []
    for arrs, per_group in ((None, False), (ks, True), (vs, True)):
        for gi, (_, dil) in enumerate(DILATION_GROUPS):
            arr = q_hm if arrs is None else arrs[gi]
            nh = arr.shape[1]
            ins.append(arr.reshape(batch, nh, seq // dil, dil * HEAD_DIM))
            off = 0 if per_group else gi * HEADS_PER_GROUP
            specs.append(pl.BlockSpec((1, 1, seq // dil, dil * HEAD_DIM),
                                      lambda b, h, off=off: (b, h + off, 0, 0)))
    ins.append(bias_tab)
    specs.append(pl.BlockSpec((N_GROUPS, 1, BLK, 2 * BLK), lambda b, h: (0, h, 0, 0)))
    slab = _nbytes((seq, HEAD_DIM), BF16)
    win = 10 * slab + _nbytes((N_GROUPS, BLK, 2 * BLK), F32) + 3 * _nbytes((seq, HEAD_DIM), F32)
    return pl.pallas_call(
        functools.partial(_attn_kernel, seq=seq),
        grid=(batch, HEADS_PER_GROUP),
        in_specs=specs,
        out_specs=pl.BlockSpec((1, seq, HEAD_DIM), lambda b, h: (b, 0, h)),
        out_shape=jax.ShapeDtypeStruct((batch, seq, ATTN_OUT), BF16),
        scratch_shapes=[pltpu.VMEM((seq, HEAD_DIM), F32)] * 6,
        compiler_params=_params(("parallel", "arbitrary"), win),
        name="dilated_attention",
    )(*ins)


def _sgu_kernel(u_ref, gv_ref, g_ref, w_ref, bt_ref, o_ref, vs_sc):
    gv = gv_ref[...].astype(F32)
    vs_sc[...] = (_rms(gv) * g_ref[...]).astype(BF16)
    tm = gv.shape[0]
    row = lax.broadcasted_iota(jnp.int32, (BLK, BLK), 0)
    col = lax.broadcasted_iota(jnp.int32, (BLK, BLK), 1)
    tril = (row >= col).astype(F32)
    for g in range(GM_GROUPS):
        cols = slice(g * BLK, (g + 1) * BLK)
        wg = (w_ref[g] * tril).astype(BF16)
        bg = bt_ref[:, g:g + 1]
        for c in range(tm // BLK):
            rows = slice(c * BLK, (c + 1) * BLK)
            mix = _dot(wg, vs_sc[rows, cols]) + bg
            o_ref[rows, cols] = (u_ref[rows, cols].astype(F32) * mix).astype(o_ref.dtype)


def _sgu(uv, g_sgu, w_sgu, b_sgu, tm):
    m, c = uv.shape[0], GM_WIDTH
    win = 3 * _nbytes((tm, c), BF16) + _nbytes((GM_GROUPS, BLK, BLK), F32)
    return pl.pallas_call(
        _sgu_kernel,
        grid=(m // tm,),
        in_specs=[pl.BlockSpec((tm, c), lambda i: (i, 0)),
                  pl.BlockSpec((tm, c), lambda i: (i, 1)),
                  pl.BlockSpec((1, c), lambda i: (0, 0)),
                  pl.BlockSpec((GM_GROUPS, BLK, BLK), lambda i: (0, 0, 0)),
                  pl.BlockSpec((BLK, GM_GROUPS), lambda i: (0, 0))],
        out_specs=pl.BlockSpec((tm, c), lambda i: (i, 0)),
        out_shape=jax.ShapeDtypeStruct((m, c), BF16),
        scratch_shapes=[pltpu.VMEM((tm, c), BF16)],
        compiler_params=_params(("parallel",), win),
        name="spatial_gating",
    )(uv, uv, g_sgu.reshape(1, c), w_sgu, b_sgu.T)


def _merge_kernel(a_ref, s_ref, wa_ref, wb_ref, ga_ref, gb_ref, o_ref):
    a = _dot(a_ref[...], wa_ref[...])
    b = _dot(s_ref[...], wb_ref[...])
    o_ref[...] = (ga_ref[...].astype(F32) * a + gb_ref[...].astype(F32) * b).astype(o_ref.dtype)


def _merge(attn, sgu, wa, wb, ga, gb, gb_col0, out_dtype, tm):
    m = attn.shape[0]
    win = (_nbytes((tm, ATTN_OUT), attn.dtype) + _nbytes((tm, GM_WIDTH), sgu.dtype)
           + _nbytes((ATTN_OUT + GM_WIDTH, TN), BF16) + 2 * _nbytes((tm, TN), ga.dtype)
           + _nbytes((tm, TN), out_dtype))
    row = lambda width: pl.BlockSpec((tm, width), lambda i, j: (i, 0))
    tile = pl.BlockSpec((tm, TN), lambda i, j: (i, j))
    return pl.pallas_call(
        _merge_kernel,
        grid=(m // tm, D_MODEL // TN),
        in_specs=[row(ATTN_OUT), row(GM_WIDTH),
                  pl.BlockSpec((ATTN_OUT, TN), lambda i, j: (0, j)),
                  pl.BlockSpec((GM_WIDTH, TN), lambda i, j: (0, j)), tile,
                  pl.BlockSpec((tm, TN), lambda i, j: (i, j + gb_col0))],
        out_specs=tile,
        out_shape=jax.ShapeDtypeStruct((m, D_MODEL), out_dtype),
        compiler_params=_params(("parallel", "arbitrary"), win),
        name="branch_merge",
    )(attn, sgu, wa, wb, ga, gb)


def _mm_res_kernel(x_ref, w_ref, r_ref, o_ref):
    part = _dot(x_ref[...], w_ref[...])

    @pl.when(pl.program_id(2) == 0)
    def _():
        o_ref[...] = r_ref[...] + part

    @pl.when(pl.program_id(2) > 0)
    def _():
        o_ref[...] += part


def _mm_res(x, w, res, tm, tk):
    m, k = x.shape
    n = w.shape[1]
    win = _nbytes((tm, tk), x.dtype) + _nbytes((tk, TN), BF16) + 2 * _nbytes((tm, TN), F32)
    return pl.pallas_call(
        _mm_res_kernel,
        grid=(m // tm, n // TN, k // tk),
        in_specs=[pl.BlockSpec((tm, tk), lambda i, j, kk: (i, kk)),
                  pl.BlockSpec((tk, TN), lambda i, j, kk: (kk, j)),
                  pl.BlockSpec((tm, TN), lambda i, j, kk: (i, j))],
        out_specs=pl.BlockSpec((tm, TN), lambda i, j, kk: (i, j)),
        out_shape=jax.ShapeDtypeStruct((m, n), F32),
        compiler_params=_params(("parallel", "parallel", "arbitrary"), win),
        name="matmul_residual",
    )(x, w, res)


FFN_TN = 512


def _glu_kernel(x_ref, wg_ref, wu_ref, o_ref):
    x = x_ref[...]
    o_ref[...] = (_silu(_dot(x, wg_ref[...])) * _dot(x, wu_ref[...])).astype(o_ref.dtype)


def _glu(h, wg, wu, out_dtype, tm):
    m, k = h.shape
    n = wg.shape[1]
    win = _nbytes((tm, k), h.dtype) + 2 * _nbytes((k, FFN_TN), BF16) + _nbytes((tm, FFN_TN), out_dtype)
    wspec = pl.BlockSpec((k, FFN_TN), lambda i, j: (0, j))
    return pl.pallas_call(
        _glu_kernel,
        grid=(m // tm, n // FFN_TN),
        in_specs=[pl.BlockSpec((tm, k), lambda i, j: (i, 0)), wspec, wspec],
        out_specs=pl.BlockSpec((tm, FFN_TN), lambda i, j: (i, j)),
        out_shape=jax.ShapeDtypeStruct((m, n), out_dtype),
        compiler_params=_params(("parallel", "arbitrary"), win),
        name="swiglu_up",
    )(h, wg, wu)


def _ple_kernel(h_ref, p_ref, wg_ref, wp_ref, r_ref, o_ref):
    gate = jax.nn.sigmoid(_dot(h_ref[...], wg_ref[...]))
    o_ref[...] = r_ref[...] + gate * _dot(p_ref[...], wp_ref[...])


def _ple(h, p, wg, wp, res, tm):
    m, k = h.shape
    tn = FFN_TN
    win = (_nbytes((tm, k), h.dtype) + _nbytes((tm, PLE_DIM), F32) + _nbytes((k + PLE_DIM, tn), BF16)
           + 2 * _nbytes((tm, tn), F32))
    tile = pl.BlockSpec((tm, tn), lambda i, j: (i, j))
    return pl.pallas_call(
        _ple_kernel,
        grid=(m // tm, D_MODEL // tn),
        in_specs=[pl.BlockSpec((tm, k), lambda i, j: (i, 0)),
                  pl.BlockSpec((tm, PLE_DIM), lambda i, j: (i, 0)),
                  pl.BlockSpec((k, tn), lambda i, j: (0, j)),
                  pl.BlockSpec((PLE_DIM, tn), lambda i, j: (0, j)), tile],
        out_specs=tile,
        out_shape=jax.ShapeDtypeStruct((m, D_MODEL), F32),
        compiler_params=_params(("parallel", "arbitrary"), win),
        name="ple_gate",
    )(h, p, wg, wp, res)


def _sample_mix_kernel(zh_ref, zr_ref, c0_ref, c1_ref, c2_ref, gq_ref, gk_ref, gsgu_ref,
                       w00_ref, b0_ref, bias_ref, bias0_ref,
                       kvn_ref, attn_ref, sguv_ref, sguo_ref, ga_ref, gb_ref, qn_sc, kn_sc):
    nh = N_ATTN_HEADS
    qn_sc[...] = (_rms(zh_ref[0, 0:nh, :]) * gq_ref[...]) * Q_SCALE
    kn_sc[...] = _rms(zh_ref[0, nh:2 * nh, :]) * gk_ref[...]
    for gi in range(N_GROUPS):
        hs = slice(gi * HEADS_PER_GROUP, (gi + 1) * HEADS_PER_GROUP)
        kvn_ref[0, gi, 0] = kn_sc[hs, :]
        kvn_ref[0, gi, 1] = zh_ref[0, 2 * nh + gi * HEADS_PER_GROUP:2 * nh + (gi + 1) * HEADS_PER_GROUP, :]

    caches = (c0_ref, c1_ref, c2_ref)
    for hh in range(HEADS_PER_GROUP):
        outs, lses = [], []
        for gi in range(N_GROUPS):
            head = gi * HEADS_PER_GROUP + hh
            c_ref = caches[gi]
            kc = c_ref[0, 0, :, hh * HEAD_DIM:(hh + 1) * HEAD_DIM]
            vc = c_ref[0, 0, :, ATTN_OUT + hh * HEAD_DIM:ATTN_OUT + (hh + 1) * HEAD_DIM]
            qh = qn_sc[pl.ds(head, 1), :]
            k_new = kn_sc[pl.ds(head, 1), :]
            v_new = zh_ref[0, pl.ds(2 * nh + head, 1), :]
            s = jnp.sum(kc * qh, axis=-1, keepdims=True) + bias_ref[gi, :, hh:hh + 1]
            s_new = jnp.sum(qh * k_new, axis=-1, keepdims=True) + bias0_ref[gi, :, hh:hh + 1]
            m = jnp.maximum(jnp.max(s, axis=0, keepdims=True), s_new)
            e = jnp.exp(s - m)
            e_new = jnp.exp(s_new - m)
            den = jnp.sum(e, axis=0, keepdims=True) + e_new
            outs.append((jnp.sum(e * vc, axis=0, keepdims=True) + e_new * v_new) / den)
            lses.append(m + jnp.log(den))
        top = jnp.maximum(jnp.maximum(lses[0], lses[1]), lses[2])
        wts = [jnp.exp(l - top) for l in lses]
        mixed = (wts[0] * outs[0] + wts[1] * outs[1] + wts[2] * outs[2]) / (wts[0] + wts[1] + wts[2])
        attn_ref[0, pl.ds(hh, 1), :] = mixed

    c = GM_WIDTH
    u = _gelu(zr_ref[0, :, 0:c])
    vs = _rms(_gelu(zr_ref[0, :, c:2 * c])) * gsgu_ref[...]
    sguv_ref[0] = vs
    sguo_ref[0] = u * (w00_ref[...] * vs + b0_ref[...])
    ga_ref[0] = jax.nn.sigmoid(zr_ref[0, :, 2 * c:2 * c + D_MODEL])
    gb_ref[0] = jax.nn.sigmoid(zr_ref[0, :, 2 * c + D_MODEL:2 * c + 2 * D_MODEL])


def _sample_mix(z, caches, layer, g_q, g_k, g_sgu, w_sgu, b_sgu, bias_s, bias0_s):
    nb = z.shape[0]
    zh = z[:, :3 * ATTN_WIDTH].reshape(nb, 3 * N_ATTN_HEADS, HEAD_DIM)
    rest = 2 * GM_WIDTH + 2 * D_MODEL
    zr = z[:, 3 * ATTN_WIDTH:].reshape(nb, 1, rest)
    w00 = jnp.repeat(w_sgu[:, 0, 0], BLK).reshape(1, GM_WIDTH)
    b0 = jnp.repeat(b_sgu[:, 0], BLK).reshape(1, GM_WIDTH)
    row = 2 * ATTN_OUT
    cache_views, cache_specs = [], []
    for c, (_, dil) in zip(caches, DILATION_GROUPS):
        depth, _, length = c.shape[:3]
        cache_views.append(c.reshape(depth, nb, length // dil, dil * row))
        cache_specs.append(pl.BlockSpec((1, 1, BLK, row), lambda b: (layer, b, 0, 0)))
    vec = lambda width: pl.BlockSpec((1, width), lambda b: (0, 0))
    per_b = lambda width: pl.BlockSpec((1, 1, width), lambda b: (b, 0, 0))
    win = 3 * _nbytes((BLK, row), F32) + _nbytes((1, 4 * rest), F32)
    return pl.pallas_call(
        _sample_mix_kernel,
        grid=(nb,),
        in_specs=[pl.BlockSpec((1, 3 * N_ATTN_HEADS, HEAD_DIM), lambda b: (b, 0, 0)), per_b(rest),
                  *cache_specs, vec(HEAD_DIM), vec(HEAD_DIM), vec(GM_WIDTH), vec(GM_WIDTH),
                  vec(GM_WIDTH),
                  pl.BlockSpec((N_GROUPS, BLK, HEADS_PER_GROUP), lambda b: (0, 0, 0)),
                  pl.BlockSpec((N_GROUPS, 1, HEADS_PER_GROUP), lambda b: (0, 0, 0))],
        out_specs=[pl.BlockSpec((1, N_GROUPS, 2, HEADS_PER_GROUP, HEAD_DIM), lambda b: (b, 0, 0, 0, 0)),
                   pl.BlockSpec((1, HEADS_PER_GROUP, HEAD_DIM), lambda b: (b, 0, 0)),
                   per_b(GM_WIDTH), per_b(GM_WIDTH), per_b(D_MODEL), per_b(D_MODEL)],
        out_shape=[jax.ShapeDtypeStruct((nb, N_GROUPS, 2, HEADS_PER_GROUP, HEAD_DIM), F32),
                   jax.ShapeDtypeStruct((nb, HEADS_PER_GROUP, HEAD_DIM), F32),
                   jax.ShapeDtypeStruct((nb, 1, GM_WIDTH), F32),
                   jax.ShapeDtypeStruct((nb, 1, GM_WIDTH), F32),
                   jax.ShapeDtypeStruct((nb, 1, D_MODEL), F32),
                   jax.ShapeDtypeStruct((nb, 1, D_MODEL), F32)],
        scratch_shapes=[pltpu.VMEM((N_ATTN_HEADS, HEAD_DIM), F32)] * 2,
        compiler_params=_params(("parallel",), win),
        name="decode_mixers",
    )(zh, zr, *cache_views, g_q.reshape(1, HEAD_DIM), g_k.reshape(1, HEAD_DIM),
      g_sgu.reshape(1, GM_WIDTH), w00, b0, bias_s, bias0_s)


def _t5_bucket(dist):
    dist = np.asarray(dist)
    max_exact = NUM_BUCKETS // 2
    large = max_exact + (np.log(np.maximum(dist, 1) / max_exact) / np.log(REL_MAX_DIST / max_exact)
                         * (NUM_BUCKETS - max_exact)).astype(np.int32)
    large = np.minimum(large, NUM_BUCKETS - 1)
    return np.where(dist < max_exact, dist, large).astype(np.int32)


def _bias_tables(rel_bias):
    qi = np.arange(BLK)[:, None]
    kj = np.arange(2 * BLK)[None, :]
    off = qi + BLK - kj
    valid = (off >= 0) & (off <= BLK)
    prompt, dec, dec0 = [], [], []
    for gi, (_, dil) in enumerate(DILATION_GROUPS):
        heads = rel_bias[:, gi * HEADS_PER_GROUP:(gi + 1) * HEADS_PER_GROUP].astype(F32)
        tab = jnp.take(heads, _t5_bucket(np.clip(off, 0, BLK) * dil).reshape(-1), axis=0)
        tab = tab.reshape(BLK, 2 * BLK, HEADS_PER_GROUP)
        tab = jnp.where(valid[:, :, None], tab, NEG_INF)
        prompt.append(jnp.transpose(tab, (2, 0, 1)))
        dec.append(jnp.take(heads, _t5_bucket((BLK - np.arange(BLK)) * dil), axis=0))
        dec0.append(heads[0:1])
    return jnp.stack(prompt), jnp.stack(dec), jnp.stack(dec0)


def _layer_weights(i, w_in, w_attn_br, w_sgu_br, w_out, w_ffn_gate, w_ffn_up, w_ffn_down,
                   w_ple_gate, w_ple_proj):
    pad = D_FF_PAD - D_FF
    return dict(
        w_in=w_in[i].astype(BF16), w_attn_br=w_attn_br[i].astype(BF16),
        w_sgu_br=w_sgu_br[i].astype(BF16), w_out=w_out[i].astype(BF16),
        w_gate=jnp.pad(w_ffn_gate[i].astype(BF16), ((0, 0), (0, pad))),
        w_up=jnp.pad(w_ffn_up[i].astype(BF16), ((0, 0), (0, pad))),
        w_down=jnp.pad(w_ffn_down[i].astype(BF16), ((0, pad), (0, 0))),
        w_ple_gate=w_ple_gate[i].astype(BF16), w_ple_proj=w_ple_proj[i].astype(BF16))


def _tail(x, p, merged_in, wts, g_ffn, g_ple, tm, act_dtype):
    attn, sgu, ga, gb, gb_col0 = merged_in
    merged = _merge(attn, sgu, wts["w_attn_br"], wts["w_sgu_br"], ga, gb, gb_col0, act_dtype, tm)
    x = _mm_res(merged, wts["w_out"], x, tm, D_MODEL)
    h2 = _rmsnorm(x, g_ffn, act_dtype, min(tm, 256))
    ff = _glu(h2, wts["w_gate"], wts["w_up"], act_dtype, tm)
    x = _mm_res(ff, wts["w_down"], x, tm, D_FF_PAD // 4)
    h3 = _rmsnorm(x, g_ple, act_dtype, min(tm, 256))
    return _ple(h3, p, wts["w_ple_gate"], wts["w_ple_proj"], x, tm)


def _prompt_layer(x, p, wts, bias_tab, g_mix, g_q, g_k, g_sgu, w_sgu, b_sgu, g_ffn, g_ple,
                  batch, seq):
    tm = 1024
    h = _rmsnorm(x, g_mix, BF16, 256)
    q_hm = _q_proj(h, wts["w_in"], g_q, batch, seq, tm)
    ks, vs, kvs = [], [], []
    for gi in range(N_GROUPS):
        k_hm, v_hm, kv = _kv_proj(h, wts["w_in"], g_k, gi, batch, seq, 512)
        ks.append(k_hm)
        vs.append(v_hm)
        kvs.append(kv)
    attn = _attention(q_hm, ks, vs, bias_tab, batch, seq).reshape(batch * seq, ATTN_OUT)
    uv = _proj_act(h, wts["w_in"], COL_U, 4, "gelu", BF16, tm)
    gates = _proj_act(h, wts["w_in"], COL_GATE, 8, "sigmoid", BF16, tm)
    sgu = _sgu(uv, g_sgu, w_sgu, b_sgu, 512)
    x = _tail(x, p, (attn, sgu, gates, gates, D_MODEL // TN), wts, g_ffn, g_ple, tm, BF16)
    return x, kvs


def _sample_layer(x, p, caches, layer, wts, bias_s, bias0_s, g_mix, g_q, g_k, g_sgu, w_sgu, b_sgu,
                  g_ffn, g_ple):
    nb = x.shape[0]
    h = _rmsnorm(x, g_mix, F32, nb)
    z = _proj_act(h, wts["w_in"], 0, N_IN_TILES, "none", F32, nb)
    kvn, attn, sguv, sguo, ga, gb = _sample_mix(z, caches, layer, g_q, g_k, g_sgu, w_sgu, b_sgu,
                                                bias_s, bias0_s)
    mixers = (attn.reshape(nb, ATTN_OUT), sguo.reshape(nb, GM_WIDTH),
              ga.reshape(nb, D_MODEL), gb.reshape(nb, D_MODEL), 0)
    x = _tail(x, p, mixers, wts, g_ffn, g_ple, nb, F32)
    return x, kvn, sguv


def kernel(x_prompt, x_sample, cache_kv_w128, cache_kv_w512, cache_kv_w2048, p_prompt, p_sample,
           rel_bias, g_mix, w_in, g_q, g_k, g_sgu, w_sgu, b_sgu, w_attn_br, w_sgu_br, w_out, g_ffn,
           w_ffn_gate, w_ffn_up, w_ffn_down, g_ple, w_ple_gate, w_ple_proj):
    batch, seq, _ = x_prompt.shape
    nb = x_sample.shape[0]
    depth = w_in.shape[0]
    caches = (cache_kv_w128, cache_kv_w512, cache_kv_w2048)
    bias_tab, bias_s, bias0_s = _bias_tables(rel_bias)

    xp = x_prompt.reshape(batch * seq, D_MODEL)
    xs = x_sample.reshape(nb, D_MODEL)
    kv_prompt = [[] for _ in range(N_GROUPS)]
    kv_new = [[] for _ in range(N_GROUPS)]
    sgu_v = []
    for i in range(depth):
        wts = _layer_weights(i, w_in, w_attn_br, w_sgu_br, w_out, w_ffn_gate, w_ffn_up, w_ffn_down,
                             w_ple_gate, w_ple_proj)
        norms = (g_mix[i], g_q[i], g_k[i], g_sgu[i], w_sgu[i], b_sgu[i], g_ffn[i], g_ple[i])
        xp, kvs = _prompt_layer(xp, p_prompt[i].reshape(batch * seq, PLE_DIM), wts, bias_tab, *norms,
                                batch, seq)
        xs, kvn, sguv = _sample_layer(xs, p_sample[i].reshape(nb, PLE_DIM), caches, i, wts,
                                      bias_s, bias0_s, *norms)
        for gi in range(N_GROUPS):
            kv_prompt[gi].append(kvs[gi].reshape(batch, -1, 2, HEADS_PER_GROUP, HEAD_DIM))
            kv_new[gi].append(kvn[:, gi])
        sgu_v.append(sguv)

    kv_p = [jnp.stack(layers, axis=0) for layers in kv_prompt]
    kv_s = [jnp.concatenate([c[:, :, 1:], jnp.stack(kv_new[gi], axis=0)[:, :, None]], axis=2)
            for gi, c in enumerate(caches)]
    return (xp.reshape(batch, seq, D_MODEL), xs.reshape(nb, 1, D_MODEL), *kv_p, *kv_s,
            jnp.stack(sgu_v, axis=0))
```

```python
import functools

import numpy as np
import jax
import jax.numpy as jnp
from jax import lax
from jax.experimental import pallas as pl
from jax.experimental.pallas import tpu as pltpu

D_MODEL = 4096
HEAD_DIM = 128
DILATION_GROUPS = ((128, 1), (512, 4), (2048, 16))
N_GROUPS = 3
HEADS_PER_GROUP = 8
N_ATTN_HEADS = 24
ATTN_WIDTH = 3072
ATTN_OUT = 1024
BLK = 128
KV_ROWS = 2 * HEADS_PER_GROUP
GM_WIDTH = 2048
GM_GROUPS = 16
D_FF = 11008
D_FF_PAD = 11264
PLE_DIM = 256
NUM_BUCKETS = 32
REL_MAX_DIST = 2048
NORM_EPS = 1e-6
NEG_INF = -1e30
Q_SCALE = HEAD_DIM ** -0.5

TN = 1024
COL_Q, COL_K, COL_V, COL_U, COL_GATE = 0, 3, 6, 9, 13
N_IN_TILES = 21

V7X_VMEM_REQUEST_CAP = 60000 * 1024
COMPILER_TEMP_BYTES = 10 * 1024 * 1024

F32 = jnp.float32
BF16 = jnp.bfloat16


def _params(sems, window_bytes):
    limit = min(V7X_VMEM_REQUEST_CAP, 2 * window_bytes + COMPILER_TEMP_BYTES)
    return pltpu.CompilerParams(dimension_semantics=sems, vmem_limit_bytes=int(limit))


def _nbytes(shape, dtype):
    return int(np.prod(shape)) * jnp.dtype(dtype).itemsize


def _gelu(x):
    return x * (0.5 * (1.0 + jnp.tanh(0.7978845608028654 * (x + 0.044715 * (x * x * x)))))


def _silu(x):
    return x * jax.nn.sigmoid(x)


_ACT = {"gelu": _gelu, "sigmoid": jax.nn.sigmoid, "none": lambda x: x}


def _rms(x):
    return x * lax.rsqrt(jnp.mean(x * x, axis=-1, keepdims=True) + NORM_EPS)


def _dot(a, b):
    return jnp.dot(a.astype(BF16), b.astype(BF16), preferred_element_type=F32)


def _dot_nt(a, b):
    return lax.dot_general(a, b, (((1,), (1,)), ((), ())), preferred_element_type=F32)


def _wspec(layer, k, tn, col_map):
    return pl.BlockSpec((None, k, tn), lambda *ids: (layer, 0, col_map(*ids)))


def _rmsnorm_kernel(x_ref, g_ref, o_ref):
    o_ref[...] = (_rms(x_ref[...]) * g_ref[...]).astype(o_ref.dtype)


def _rmsnorm(x, g, out_dtype, tm):
    m, d = x.shape
    return pl.pallas_call(
        _rmsnorm_kernel,
        grid=(m // tm,),
        in_specs=[pl.BlockSpec((tm, d), lambda i: (i, 0)), pl.BlockSpec((1, d), lambda i: (0, 0))],
        out_specs=pl.BlockSpec((tm, d), lambda i: (i, 0)),
        out_shape=jax.ShapeDtypeStruct((m, d), out_dtype),
        compiler_params=_params(("parallel",), _nbytes((tm, d), F32) + _nbytes((tm, d), out_dtype)),
        name="rmsnorm",
    )(x, g.reshape(1, d))


def _proj_act_kernel(x_ref, w_ref, o_ref, *, act):
    o_ref[...] = _ACT[act](_dot(x_ref[...], w_ref[...])).astype(o_ref.dtype)


def _proj_act(h, w, layer, col0, n_tiles, act, out_dtype, tm):
    m, k = h.shape
    win = _nbytes((tm, k), h.dtype) + _nbytes((k, TN), BF16) + _nbytes((tm, TN), out_dtype)
    return pl.pallas_call(
        functools.partial(_proj_act_kernel, act=act),
        grid=(m // tm, n_tiles),
        in_specs=[pl.BlockSpec((tm, k), lambda i, j: (i, 0)),
                  _wspec(layer, k, TN, lambda i, j: j + col0)],
        out_specs=pl.BlockSpec((tm, TN), lambda i, j: (i, j)),
        out_shape=jax.ShapeDtypeStruct((m, n_tiles * TN), out_dtype),
        compiler_params=_params(("parallel", "arbitrary"), win),
        name="proj_" + act,
    )(h, w)


def _q_kernel(x_ref, w_ref, g_ref, o_ref):
    acc = _dot(x_ref[...], w_ref[...])
    for hh in range(HEADS_PER_GROUP):
        a = acc[:, hh * HEAD_DIM:(hh + 1) * HEAD_DIM]
        o_ref[0, hh] = ((_rms(a) * g_ref[...]) * Q_SCALE).astype(o_ref.dtype)


def _q_proj(h, w, layer, g_q, batch, seq, tm):
    m, k = h.shape
    tpb = seq // tm
    win = _nbytes((tm, k), BF16) + _nbytes((k, TN), BF16) + _nbytes((tm, TN), BF16)
    return pl.pallas_call(
        _q_kernel,
        grid=(m // tm, N_GROUPS),
        in_specs=[pl.BlockSpec((tm, k), lambda i, j: (i, 0)),
                  _wspec(layer, k, TN, lambda i, j: j + COL_Q),
                  pl.BlockSpec((1, HEAD_DIM), lambda i, j: (0, 0))],
        out_specs=pl.BlockSpec((1, HEADS_PER_GROUP, tm, HEAD_DIM),
                               lambda i, j: (i // tpb, j, i % tpb, 0)),
        out_shape=jax.ShapeDtypeStruct((batch, N_ATTN_HEADS, seq, HEAD_DIM), BF16),
        compiler_params=_params(("parallel", "arbitrary"), win),
        name="q_proj",
    )(h, w, g_q.reshape(1, HEAD_DIM))


def _kv_kernel(x_ref, wk_ref, wv_ref, g_ref, kv_prev_ref, khm_ref, vhm_ref, kv_ref, *, rows):
    del kv_prev_ref
    x = x_ref[...]
    k = _dot(x, wk_ref[...])
    v = _dot(x, wv_ref[...])
    tm = x.shape[0]
    for hh in range(HEADS_PER_GROUP):
        sl = slice(hh * HEAD_DIM, (hh + 1) * HEAD_DIM)
        kn = _rms(k[:, sl]) * g_ref[...]
        vh = v[:, sl]
        khm_ref[0, hh] = kn.astype(BF16)
        vhm_ref[0, hh] = vh.astype(BF16)
        kv_ref[pl.ds(hh, rows, stride=KV_ROWS), :] = kn[tm - rows:]
        kv_ref[pl.ds(HEADS_PER_GROUP + hh, rows, stride=KV_ROWS), :] = vh[tm - rows:]


def _kv_proj(h, w, layer, depth, g_k, gi, kv_prev, batch, seq, tm):
    m, k = h.shape
    tpb = seq // tm
    keep = min(DILATION_GROUPS[gi][0], seq)
    rows = min(keep, tm)
    kv_tiles = keep // rows
    kv_map = lambda i: (layer, i // tpb, jnp.maximum(i % tpb - (tpb - kv_tiles), 0), 0)
    hm_spec = pl.BlockSpec((1, HEADS_PER_GROUP, tm, HEAD_DIM), lambda i: (i // tpb, 0, i % tpb, 0))
    hm_shape = jax.ShapeDtypeStruct((batch, HEADS_PER_GROUP, seq, HEAD_DIM), BF16)
    kv_shape = jax.ShapeDtypeStruct((depth, batch, keep * KV_ROWS, HEAD_DIM), F32)
    win = (_nbytes((tm, k), BF16) + _nbytes((k, TN), BF16) + 2 * _nbytes((tm, TN), BF16)
           + _nbytes((rows * KV_ROWS, HEAD_DIM), F32))
    if kv_prev is None:
        kv_prev = jnp.zeros(kv_shape.shape, F32)
    return pl.pallas_call(
        functools.partial(_kv_kernel, rows=rows),
        grid=(m // tm,),
        in_specs=[pl.BlockSpec((tm, k), lambda i: (i, 0)),
                  pl.BlockSpec((None, k, TN), lambda i: (layer, 0, COL_K + gi), pipeline_mode=pl.Buffered(1)),
                  pl.BlockSpec((None, k, TN), lambda i: (layer, 0, COL_V + gi), pipeline_mode=pl.Buffered(1)),
                  pl.BlockSpec((1, HEAD_DIM), lambda i: (0, 0)),
                  pl.BlockSpec(memory_space=pl.ANY)],
        out_specs=[hm_spec, hm_spec, pl.BlockSpec((None, None, rows * KV_ROWS, HEAD_DIM), kv_map)],
        out_shape=[hm_shape, hm_shape, kv_shape],
        input_output_aliases={4: 2},
        compiler_params=_params(("arbitrary",), win),
        name="kv_proj",
    )(h, w, w, g_k.reshape(1, HEAD_DIM), kv_prev)


ATTN_BATCH = 4


def _attn_blocks(qkv, bias):
    scores = [_dot_nt(q, k) + bias for q, k, _ in qkv]
    tops = [jnp.max(s, axis=-1, keepdims=True) for s in scores]
    probs = [jnp.exp(s - m).astype(BF16) for s, m in zip(scores, tops)]
    nds = [jnp.dot(e, jnp.concatenate([v, jnp.ones_like(v)], axis=1), preferred_element_type=F32)
           for e, (_, _, v) in zip(probs, qkv)]
    return [(nd[:, :HEAD_DIM] / nd[:, HEAD_DIM:], m + jnp.log(nd[:, HEAD_DIM:])) for nd, m in zip(nds, tops)]


def _attn_kernel(q0_ref, q1_ref, q2_ref, k0_ref, k1_ref, k2_ref, v0_ref, v1_ref, v2_ref, bias_ref, o_ref,
                 q1f, q2f, k1f, k2f, v1f, v2f, o0_sc, o1_sc, o2_sc, l0_sc, l1_sc, l2_sc, *, seq):
    for src, dst in ((q1_ref, q1f), (q2_ref, q2f), (k1_ref, k1f), (k2_ref, k2f), (v1_ref, v1f), (v2_ref, v2f)):
        dst[...] = src[0, 0].astype(F32)

    def run_group(gi, load, o_sc, l_sc, unroll):
        dil = DILATION_GROUPS[gi][1]
        nblk = seq // dil // BLK

        def rows(r, s0, n):
            if dil == 1:
                return pl.ds(s0 if isinstance(s0, int) else pl.multiple_of(s0, BLK), n)
            return pl.ds(s0 * dil + r, n, stride=dil)

        def blocks(todo, first):
            back = 0 if first else BLK
            bias = bias_ref[gi, 0, :, BLK:2 * BLK] if first else bias_ref[gi, 0]
            qkv = [load(rows(r, s0, BLK), rows(r, s0 - back, BLK + back)) for r, s0 in todo]
            for (r, s0), (o, lse) in zip(todo, _attn_blocks(qkv, bias)):
                o_sc[rows(r, s0, BLK), :] = o
                l_sc[rows(r, s0, BLK), :] = lse

        for r0 in range(0, dil, ATTN_BATCH):
            blocks([(r, 0) for r in range(r0, min(dil, r0 + ATTN_BATCH))], True)

        def body(it, carry):
            blocks([(r, (1 + it * unroll + u) * BLK) for u in range(unroll) for r in range(dil)], False)
            return carry

        if nblk > 1:
            lax.fori_loop(0, (nblk - 1) // unroll, body, 0)

    def bf16_load(q_ref, k_ref, v_ref):
        return lambda qr, kr: (q_ref[0, 0, qr, :], k_ref[0, 0, kr, :], v_ref[0, 0, kr, :])

    def f32_load(qf, kf, vf):
        return lambda qr, kr: (qf[qr, :].astype(BF16), kf[kr, :].astype(BF16), vf[kr, :].astype(BF16))

    run_group(0, bf16_load(q0_ref, k0_ref, v0_ref), o0_sc, l0_sc, 5)
    run_group(1, f32_load(q1f, k1f, v1f), o1_sc, l1_sc, 1)
    run_group(2, f32_load(q2f, k2f, v2f), o2_sc, l2_sc, 1)

    def combine(c, carry):
        rows = pl.ds(pl.multiple_of(c * BLK, BLK), BLK)
        l0, l1, l2 = l0_sc[rows, :], l1_sc[rows, :], l2_sc[rows, :]
        top = jnp.maximum(jnp.maximum(l0, l1), l2)
        w0, w1, w2 = jnp.exp(l0 - top), jnp.exp(l1 - top), jnp.exp(l2 - top)
        mixed = (w0 * o0_sc[rows, :] + w1 * o1_sc[rows, :] + w2 * o2_sc[rows, :]) / (w0 + w1 + w2)
        o_ref[0, rows, :] = mixed.astype(o_ref.dtype)
        return carry

    lax.fori_loop(0, seq // BLK, combine, 0)


def _attention(q_hm, ks, vs, bias_tab, batch, seq):
    slab = (1, 1, seq, HEAD_DIM)
    specs = [pl.BlockSpec(slab, lambda b, h, gi=gi: (b, gi * HEADS_PER_GROUP + h, 0, 0))
             for gi in range(N_GROUPS)]
    specs += [pl.BlockSpec(slab, lambda b, h: (b, h, 0, 0))] * (2 * N_GROUPS)
    specs.append(pl.BlockSpec((N_GROUPS, 1, BLK, 2 * BLK), lambda b, h: (0, h, 0, 0)))
    win = (10 * _nbytes((seq, HEAD_DIM), BF16) + _nbytes((N_GROUPS, BLK, 2 * BLK), F32)
           + 6 * _nbytes((seq, HEAD_DIM), F32))
    return pl.pallas_call(
        functools.partial(_attn_kernel, seq=seq),
        grid=(batch, HEADS_PER_GROUP),
        in_specs=specs,
        out_specs=pl.BlockSpec((1, seq, HEAD_DIM), lambda b, h: (b, 0, h)),
        out_shape=jax.ShapeDtypeStruct((batch, seq, ATTN_OUT), BF16),
        scratch_shapes=[pltpu.VMEM((seq, HEAD_DIM), F32)] * 12,
        compiler_params=_params(("parallel", "arbitrary"), win),
        name="dilated_attention",
    )(q_hm, q_hm, q_hm, *ks, *vs, bias_tab)


def _sgu_kernel(u_ref, gv_ref, g_ref, w_ref, bt_ref, o_ref, vs_sc):
    gv = gv_ref[...].astype(F32)
    vs_sc[...] = (_rms(gv) * g_ref[...]).astype(BF16)
    tm = gv.shape[0]
    row = lax.broadcasted_iota(jnp.int32, (BLK, BLK), 0)
    col = lax.broadcasted_iota(jnp.int32, (BLK, BLK), 1)
    tril = (row >= col).astype(F32)
    for g in range(GM_GROUPS):
        cols = slice(g * BLK, (g + 1) * BLK)
        wg = (w_ref[g] * tril).astype(BF16)
        bg = bt_ref[:, g:g + 1]
        for c in range(tm // BLK):
            rows = slice(c * BLK, (c + 1) * BLK)
            mix = _dot(wg, vs_sc[rows, cols]) + bg
            o_ref[rows, cols] = (u_ref[rows, cols].astype(F32) * mix).astype(o_ref.dtype)


def _sgu(uv, g_sgu, w_sgu, b_sgu, tm):
    m, c = uv.shape[0], GM_WIDTH
    win = 3 * _nbytes((tm, c), BF16) + _nbytes((GM_GROUPS, BLK, BLK), F32)
    return pl.pallas_call(
        _sgu_kernel,
        grid=(m // tm,),
        in_specs=[pl.BlockSpec((tm, c), lambda i: (i, 0)),
                  pl.BlockSpec((tm, c), lambda i: (i, 1)),
                  pl.BlockSpec((1, c), lambda i: (0, 0)),
                  pl.BlockSpec((GM_GROUPS, BLK, BLK), lambda i: (0, 0, 0)),
                  pl.BlockSpec((BLK, GM_GROUPS), lambda i: (0, 0))],
        out_specs=pl.BlockSpec((tm, c), lambda i: (i, 0)),
        out_shape=jax.ShapeDtypeStruct((m, c), BF16),
        scratch_shapes=[pltpu.VMEM((tm, c), BF16)],
        compiler_params=_params(("parallel",), win),
        name="spatial_gating",
    )(uv, uv, g_sgu.reshape(1, c), w_sgu, b_sgu.T)


def _merge_kernel(a_ref, s_ref, wa_ref, wb_ref, ga_ref, gb_ref, o_ref):
    a = _dot(a_ref[...], wa_ref[...])
    b = _dot(s_ref[...], wb_ref[...])
    o_ref[...] = (ga_ref[...].astype(F32) * a + gb_ref[...].astype(F32) * b).astype(o_ref.dtype)


def _merge(attn, sgu, wa, wb, layer, ga, gb, gb_col0, out_dtype, tm):
    m = attn.shape[0]
    win = (_nbytes((tm, ATTN_OUT), attn.dtype) + _nbytes((tm, GM_WIDTH), sgu.dtype)
           + _nbytes((ATTN_OUT + GM_WIDTH, TN), BF16) + 2 * _nbytes((tm, TN), ga.dtype)
           + _nbytes((tm, TN), out_dtype))
    row = lambda width: pl.BlockSpec((tm, width), lambda i, j: (i, 0))
    tile = pl.BlockSpec((tm, TN), lambda i, j: (i, j))
    return pl.pallas_call(
        _merge_kernel,
        grid=(m // tm, D_MODEL // TN),
        in_specs=[row(ATTN_OUT), row(GM_WIDTH),
                  _wspec(layer, ATTN_OUT, TN, lambda i, j: j),
                  _wspec(layer, GM_WIDTH, TN, lambda i, j: j), tile,
                  pl.BlockSpec((tm, TN), lambda i, j: (i, j + gb_col0))],
        out_specs=tile,
        out_shape=jax.ShapeDtypeStruct((m, D_MODEL), out_dtype),
        compiler_params=_params(("parallel", "arbitrary"), win),
        name="branch_merge",
    )(attn, sgu, wa, wb, ga, gb)


def _mm_res_kernel(x_ref, w_ref, r_ref, o_ref, *, k_rows):
    w = w_ref[...]
    tk = w.shape[0]
    if k_rows % tk:
        row = pl.program_id(2) * tk + lax.broadcasted_iota(jnp.int32, w.shape, 0)
        w = jnp.where(row < k_rows, w, jnp.zeros_like(w))
    part = _dot(x_ref[...], w)

    @pl.when(pl.program_id(2) == 0)
    def _():
        o_ref[...] = r_ref[...] + part

    @pl.when(pl.program_id(2) > 0)
    def _():
        o_ref[...] += part


def _mm_res(x, w, layer, res, tm, tk):
    m, k = x.shape
    k_rows, n = w.shape[1:]
    win = _nbytes((tm, tk), x.dtype) + _nbytes((tk, TN), BF16) + 2 * _nbytes((tm, TN), F32)
    return pl.pallas_call(
        functools.partial(_mm_res_kernel, k_rows=k_rows),
        grid=(m // tm, n // TN, k // tk),
        in_specs=[pl.BlockSpec((tm, tk), lambda i, j, kk: (i, kk)),
                  pl.BlockSpec((None, tk, TN), lambda i, j, kk: (layer, kk, j)),
                  pl.BlockSpec((tm, TN), lambda i, j, kk: (i, j))],
        out_specs=pl.BlockSpec((tm, TN), lambda i, j, kk: (i, j)),
        out_shape=jax.ShapeDtypeStruct((m, n), F32),
        compiler_params=_params(("parallel", "parallel", "arbitrary"), win),
        name="matmul_residual",
    )(x, w, res)


FFN_TN = 512


def _glu_kernel(x_ref, wg_ref, wu_ref, o_ref, *, n_cols):
    x = x_ref[...]
    y = _silu(_dot(x, wg_ref[...])) * _dot(x, wu_ref[...])
    col = pl.program_id(1) * y.shape[1] + lax.broadcasted_iota(jnp.int32, y.shape, 1)
    o_ref[...] = jnp.where(col < n_cols, y, 0.0).astype(o_ref.dtype)


def _glu(h, wg, wu, layer, n, out_dtype, tm):
    m, k = h.shape
    win = _nbytes((tm, k), h.dtype) + 2 * _nbytes((k, FFN_TN), BF16) + _nbytes((tm, FFN_TN), out_dtype)
    wspec = _wspec(layer, k, FFN_TN, lambda i, j: j)
    return pl.pallas_call(
        functools.partial(_glu_kernel, n_cols=wg.shape[2]),
        grid=(m // tm, n // FFN_TN),
        in_specs=[pl.BlockSpec((tm, k), lambda i, j: (i, 0)), wspec, wspec],
        out_specs=pl.BlockSpec((tm, FFN_TN), lambda i, j: (i, j)),
        out_shape=jax.ShapeDtypeStruct((m, n), out_dtype),
        compiler_params=_params(("parallel", "arbitrary"), win),
        name="swiglu_up",
    )(h, wg, wu)


def _ple_kernel(h_ref, p_ref, wg_ref, wp_ref, r_ref, o_ref):
    gate = jax.nn.sigmoid(_dot(h_ref[...], wg_ref[...]))
    o_ref[...] = r_ref[...] + gate * _dot(p_ref[...], wp_ref[...])


def _ple(h, p, wg, wp, layer, res, tm):
    m, k = h.shape
    tn = FFN_TN
    win = (_nbytes((tm, k), h.dtype) + _nbytes((tm, PLE_DIM), F32) + _nbytes((k + PLE_DIM, tn), BF16)
           + 2 * _nbytes((tm, tn), F32))
    tile = pl.BlockSpec((tm, tn), lambda i, j: (i, j))
    return pl.pallas_call(
        _ple_kernel,
        grid=(m // tm, D_MODEL // tn),
        in_specs=[pl.BlockSpec((tm, k), lambda i, j: (i, 0)),
                  pl.BlockSpec((None, tm, PLE_DIM), lambda i, j: (layer, i, 0)),
                  _wspec(layer, k, tn, lambda i, j: j),
                  _wspec(layer, PLE_DIM, tn, lambda i, j: j), tile],
        out_specs=tile,
        out_shape=jax.ShapeDtypeStruct((m, D_MODEL), F32),
        compiler_params=_params(("parallel", "arbitrary"), win),
        name="ple_gate",
    )(h, p, wg, wp, res)


def _sample_mix_kernel(zh_ref, zr_ref, c0_ref, c1_ref, c2_ref, gq_ref, gk_ref, gsgu_ref,
                       w00_ref, b0_ref, bias_ref, bias0_ref,
                       kvn_ref, attn_ref, sguv_ref, sguo_ref, ga_ref, gb_ref, qn_sc, kn_sc):
    nh, hpg = N_ATTN_HEADS, HEADS_PER_GROUP
    qn_sc[...] = (_rms(zh_ref[0, 0:nh, :]) * gq_ref[...]) * Q_SCALE
    kn_sc[...] = _rms(zh_ref[0, nh:2 * nh, :]) * gk_ref[...]

    outs, lses = [], []
    for gi, c_ref in enumerate((c0_ref, c1_ref, c2_ref)):
        hs = slice(gi * hpg, (gi + 1) * hpg)
        q, k_new = qn_sc[hs, :], kn_sc[hs, :]
        v_new = zh_ref[0, 2 * nh + gi * hpg:2 * nh + (gi + 1) * hpg, :]
        kvn_ref[0, gi, 0] = k_new
        kvn_ref[0, gi, 1] = v_new
        kc = c_ref[0, 0, :, 0:hpg, :]
        vc = c_ref[0, 0, :, hpg:2 * hpg, :]
        s = jnp.sum(kc * q[None], axis=-1, keepdims=True) + bias_ref[gi]
        s_new = jnp.sum(q * k_new, axis=-1, keepdims=True) + bias0_ref[gi]
        m = jnp.maximum(jnp.max(s, axis=0), s_new)
        e = jnp.exp(s - m[None])
        e_new = jnp.exp(s_new - m)
        den = jnp.sum(e, axis=0) + e_new
        outs.append((jnp.sum(e * vc, axis=0) + e_new * v_new) / den)
        lses.append(m + jnp.log(den))
    top = jnp.maximum(jnp.maximum(lses[0], lses[1]), lses[2])
    wts = [jnp.exp(l - top) for l in lses]
    attn_ref[0] = (wts[0] * outs[0] + wts[1] * outs[1] + wts[2] * outs[2]) / (wts[0] + wts[1] + wts[2])

    c = GM_WIDTH
    u = _gelu(zr_ref[0, :, 0:c])
    vs = _rms(_gelu(zr_ref[0, :, c:2 * c])) * gsgu_ref[...]
    sguv_ref[0] = vs
    sguo_ref[0] = u * (w00_ref[...] * vs + b0_ref[...])
    ga_ref[0] = jax.nn.sigmoid(zr_ref[0, :, 2 * c:2 * c + D_MODEL])
    gb_ref[0] = jax.nn.sigmoid(zr_ref[0, :, 2 * c + D_MODEL:2 * c + 2 * D_MODEL])


def _sample_mix(z, caches, layer, g_q, g_k, g_sgu, w_sgu, b_sgu, bias_s, bias0_s):
    nb = z.shape[0]
    hpg = HEADS_PER_GROUP
    zh = z[:, :3 * ATTN_WIDTH].reshape(nb, 3 * N_ATTN_HEADS, HEAD_DIM)
    rest = 2 * GM_WIDTH + 2 * D_MODEL
    zr = z[:, 3 * ATTN_WIDTH:].reshape(nb, 1, rest)
    w00 = jnp.repeat(w_sgu[:, 0, 0], BLK).reshape(1, GM_WIDTH)
    b0 = jnp.repeat(b_sgu[:, 0], BLK).reshape(1, GM_WIDTH)
    cache_views, cache_specs = [], []
    for c, (_, dil) in zip(caches, DILATION_GROUPS):
        depth, _, length = c.shape[:3]
        cache_views.append(c.reshape(depth, nb, length // dil, dil * KV_ROWS, HEAD_DIM))
        cache_specs.append(pl.BlockSpec((1, 1, BLK, KV_ROWS, HEAD_DIM), lambda b: (layer, b, 0, 0, 0)))
    vec = lambda width: pl.BlockSpec((1, width), lambda b: (0, 0))
    per_b = lambda width: pl.BlockSpec((1, 1, width), lambda b: (b, 0, 0))
    win = (3 * _nbytes((BLK, KV_ROWS, HEAD_DIM), F32) + _nbytes((1, 4 * rest), F32)
           + _nbytes((N_GROUPS, BLK + 1, hpg, HEAD_DIM), F32))
    return pl.pallas_call(
        _sample_mix_kernel,
        grid=(nb,),
        in_specs=[pl.BlockSpec((1, 3 * N_ATTN_HEADS, HEAD_DIM), lambda b: (b, 0, 0)), per_b(rest),
                  *cache_specs, vec(HEAD_DIM), vec(HEAD_DIM), vec(GM_WIDTH), vec(GM_WIDTH),
                  vec(GM_WIDTH),
                  pl.BlockSpec((N_GROUPS, BLK, hpg, HEAD_DIM), lambda b: (0, 0, 0, 0)),
                  pl.BlockSpec((N_GROUPS, hpg, HEAD_DIM), lambda b: (0, 0, 0))],
        out_specs=[pl.BlockSpec((1, N_GROUPS, 2, hpg, HEAD_DIM), lambda b: (b, 0, 0, 0, 0)),
                   pl.BlockSpec((1, hpg, HEAD_DIM), lambda b: (b, 0, 0)),
                   per_b(GM_WIDTH), per_b(GM_WIDTH), per_b(D_MODEL), per_b(D_MODEL)],
        out_shape=[jax.ShapeDtypeStruct((nb, N_GROUPS, 2, hpg, HEAD_DIM), F32),
                   jax.ShapeDtypeStruct((nb, hpg, HEAD_DIM), F32),
                   jax.ShapeDtypeStruct((nb, 1, GM_WIDTH), F32),
                   jax.ShapeDtypeStruct((nb, 1, GM_WIDTH), F32),
                   jax.ShapeDtypeStruct((nb, 1, D_MODEL), F32),
                   jax.ShapeDtypeStruct((nb, 1, D_MODEL), F32)],
        scratch_shapes=[pltpu.VMEM((N_ATTN_HEADS, HEAD_DIM), F32)] * 2,
        compiler_params=_params(("parallel",), win),
        name="decode_mixers",
    )(zh, zr, *cache_views, g_q.reshape(1, HEAD_DIM), g_k.reshape(1, HEAD_DIM),
      g_sgu.reshape(1, GM_WIDTH), w00, b0, bias_s, bias0_s)


def _roll_copies(cache_ref, new_ref, out_ref, sems):
    kept = cache_ref.shape[2] - KV_ROWS
    return (pltpu.make_async_copy(cache_ref.at[:, :, pl.ds(KV_ROWS, kept), :],
                                  out_ref.at[:, :, pl.ds(0, kept), :], sems.at[0]),
            pltpu.make_async_copy(new_ref, out_ref.at[:, :, pl.ds(kept, KV_ROWS), :], sems.at[1]))


def _roll_kernel(cache_ref, new_ref, out_ref, sems):
    copies = _roll_copies(cache_ref, new_ref, out_ref, sems)
    for c in copies:
        c.start()
    for c in copies:
        c.wait()


def _roll_cache(cache, new_rows):
    depth, nb, length = cache.shape[:3]
    any_spec = pl.BlockSpec(memory_space=pl.ANY)
    out = pl.pallas_call(
        _roll_kernel,
        in_specs=[any_spec, any_spec],
        out_specs=any_spec,
        out_shape=jax.ShapeDtypeStruct((depth, nb, length * KV_ROWS, HEAD_DIM), cache.dtype),
        scratch_shapes=[pltpu.SemaphoreType.DMA((2,))],
        name="roll_cache",
    )(cache.reshape(depth, nb, length * KV_ROWS, HEAD_DIM), new_rows.reshape(depth, nb, KV_ROWS, HEAD_DIM))
    return out.reshape(cache.shape)


def _t5_bucket(dist):
    dist = np.asarray(dist)
    max_exact = NUM_BUCKETS // 2
    large = max_exact + (np.log(np.maximum(dist, 1) / max_exact) / np.log(REL_MAX_DIST / max_exact)
                         * (NUM_BUCKETS - max_exact)).astype(np.int32)
    large = np.minimum(large, NUM_BUCKETS - 1)
    return np.where(dist < max_exact, dist, large).astype(np.int32)


def _bias_tables(rel_bias):
    hpg = HEADS_PER_GROUP
    prompt, dec, dec0 = [], [], []
    for gi, (_, dil) in enumerate(DILATION_GROUPS):
        heads = rel_bias[:, gi * hpg:(gi + 1) * hpg].astype(F32)
        by_off = jnp.take(heads, _t5_bucket(np.arange(BLK + 1) * dil), axis=0).T
        w = jnp.concatenate([by_off[:, ::-1], jnp.full((hpg, BLK), NEG_INF, F32)], axis=1)
        tab = jnp.tile(w, (1, BLK))[:, :BLK * 2 * BLK].reshape(hpg, BLK, 2 * BLK)
        prompt.append(tab)
        dec.append(jnp.broadcast_to(by_off[:, :0:-1].T[:, :, None], (BLK, hpg, HEAD_DIM)))
        dec0.append(jnp.broadcast_to(by_off[:, 0:1], (hpg, HEAD_DIM)))
    return jnp.stack(prompt), jnp.stack(dec), jnp.stack(dec0)


def _cast_weights(w_in, w_attn_br, w_sgu_br, w_out, w_ffn_gate, w_ffn_up, w_ffn_down,
                  w_ple_gate, w_ple_proj):
    return dict(
        w_in=w_in.astype(BF16), w_attn_br=w_attn_br.astype(BF16),
        w_sgu_br=w_sgu_br.astype(BF16), w_out=w_out.astype(BF16),
        w_gate=w_ffn_gate.astype(BF16), w_up=w_ffn_up.astype(BF16), w_down=w_ffn_down.astype(BF16),
        w_ple_gate=w_ple_gate.astype(BF16), w_ple_proj=w_ple_proj.astype(BF16))


def _tail(x, p, merged_in, wts, layer, g_ffn, g_ple, tm, act_dtype):
    attn, sgu, ga, gb, gb_col0 = merged_in
    merged = _merge(attn, sgu, wts["w_attn_br"], wts["w_sgu_br"], layer, ga, gb, gb_col0, act_dtype, tm)
    x = _mm_res(merged, wts["w_out"], layer, x, tm, D_MODEL)
    h2 = _rmsnorm(x, g_ffn, act_dtype, min(tm, 256))
    ff = _glu(h2, wts["w_gate"], wts["w_up"], layer, D_FF_PAD, act_dtype, tm)
    x = _mm_res(ff, wts["w_down"], layer, x, tm, D_FF_PAD // 4)
    h3 = _rmsnorm(x, g_ple, act_dtype, min(tm, 256))
    return _ple(h3, p, wts["w_ple_gate"], wts["w_ple_proj"], layer, x, tm)


def _prompt_layer(x, p, wts, layer, depth, kv_prev, bias_tab, g_mix, g_q, g_k, g_sgu, w_sgu, b_sgu,
                  g_ffn, g_ple, batch, seq):
    tm = 1024
    h = _rmsnorm(x, g_mix, BF16, 256)
    q_hm = _q_proj(h, wts["w_in"], layer, g_q, batch, seq, tm)
    ks, vs, kvs = [], [], []
    for gi in range(N_GROUPS):
        k_hm, v_hm, kv = _kv_proj(h, wts["w_in"], layer, depth, g_k, gi, kv_prev[gi], batch, seq, 512)
        ks.append(k_hm)
        vs.append(v_hm)
        kvs.append(kv)
    attn = _attention(q_hm, ks, vs, bias_tab, batch, seq).reshape(batch * seq, ATTN_OUT)
    uv = _proj_act(h, wts["w_in"], layer, COL_U, 4, "gelu", BF16, tm)
    gates = _proj_act(h, wts["w_in"], layer, COL_GATE, 8, "sigmoid", BF16, tm)
    sgu = _sgu(uv, g_sgu, w_sgu, b_sgu, 512)
    x = _tail(x, p, (attn, sgu, gates, gates, D_MODEL // TN), wts, layer, g_ffn, g_ple, tm, BF16)
    return x, kvs


def _sample_layer(x, p, caches, wts, layer, bias_s, bias0_s, g_mix, g_q, g_k, g_sgu, w_sgu, b_sgu,
                  g_ffn, g_ple):
    nb = x.shape[0]
    h = _rmsnorm(x, g_mix, F32, nb)
    z = _proj_act(h, wts["w_in"], layer, 0, N_IN_TILES, "none", F32, nb)
    kvn, attn, sguv, sguo, ga, gb = _sample_mix(z, caches, layer, g_q, g_k, g_sgu, w_sgu, b_sgu,
                                                bias_s, bias0_s)
    mixers = (attn.reshape(nb, ATTN_OUT), sguo.reshape(nb, GM_WIDTH),
              ga.reshape(nb, D_MODEL), gb.reshape(nb, D_MODEL), 0)
    x = _tail(x, p, mixers, wts, layer, g_ffn, g_ple, nb, F32)
    return x, kvn, sguv


def kernel(x_prompt, x_sample, cache_kv_w128, cache_kv_w512, cache_kv_w2048, p_prompt, p_sample,
           rel_bias, g_mix, w_in, g_q, g_k, g_sgu, w_sgu, b_sgu, w_attn_br, w_sgu_br, w_out, g_ffn,
           w_ffn_gate, w_ffn_up, w_ffn_down, g_ple, w_ple_gate, w_ple_proj):
    batch, seq, _ = x_prompt.shape
    nb = x_sample.shape[0]
    depth = w_in.shape[0]
    caches = (cache_kv_w128, cache_kv_w512, cache_kv_w2048)
    bias_tab, bias_s, bias0_s = _bias_tables(rel_bias)
    wts = _cast_weights(w_in, w_attn_br, w_sgu_br, w_out, w_ffn_gate, w_ffn_up, w_ffn_down,
                        w_ple_gate, w_ple_proj)

    xp = x_prompt.reshape(batch * seq, D_MODEL)
    xs = x_sample.reshape(nb, D_MODEL)
    pp = p_prompt.reshape(depth, batch * seq, PLE_DIM)
    ps = p_sample.reshape(depth, nb, PLE_DIM)
    kv_p = [None] * N_GROUPS
    kv_new = [[] for _ in range(N_GROUPS)]
    sgu_v = []
    for i in range(depth):
        norms = (g_mix[i], g_q[i], g_k[i], g_sgu[i], w_sgu[i], b_sgu[i], g_ffn[i], g_ple[i])
        xp, kv_p = _prompt_layer(xp, pp, wts, i, depth, kv_p, bias_tab, *norms, batch, seq)
        xs, kvn, sguv = _sample_layer(xs, ps, caches, wts, i, bias_s, bias0_s, *norms)
        for gi in range(N_GROUPS):
            kv_new[gi].append(kvn[:, gi])
        sgu_v.append(sguv)

    kv_p = [kv.reshape(depth, batch, -1, 2, HEADS_PER_GROUP, HEAD_DIM) for kv in kv_p]
    kv_s = [_roll_cache(c, jnp.stack(kv_new[gi], axis=0)) for gi, c in enumerate(caches)]
    return (xp.reshape(batch, seq, D_MODEL), xs.reshape(nb, 1, D_MODEL), *kv_p, *kv_s,
            jnp.stack(sgu_v, axis=0))
```

```python
import functools

import numpy as np
import jax
import jax.numpy as jnp
from jax import lax
from jax.experimental import pallas as pl
from jax.experimental.pallas import tpu as pltpu

D_MODEL = 4096
HEAD_DIM = 128
DILATION_GROUPS = ((128, 1), (512, 4), (2048, 16))
N_GROUPS = 3
HEADS_PER_GROUP = 8
N_ATTN_HEADS = 24
ATTN_WIDTH = 3072
ATTN_OUT = 1024
BLK = 128
KV_ROWS = 2 * HEADS_PER_GROUP
GM_WIDTH = 2048
GM_GROUPS = 16
D_FF = 11008
D_FF_PAD = 11264
PLE_DIM = 256
NUM_BUCKETS = 32
REL_MAX_DIST = 2048
NORM_EPS = 1e-6
NEG_INF = -1e30
Q_SCALE = HEAD_DIM ** -0.5

TN = 1024
COL_Q, COL_K, COL_V, COL_U, COL_GATE = 0, 3, 6, 9, 13
N_IN_TILES = 21

V7X_VMEM_REQUEST_CAP = 60000 * 1024
COMPILER_TEMP_BYTES = 10 * 1024 * 1024

F32 = jnp.float32
BF16 = jnp.bfloat16


def _params(sems, window_bytes):
    limit = min(V7X_VMEM_REQUEST_CAP, 2 * window_bytes + COMPILER_TEMP_BYTES)
    return pltpu.CompilerParams(dimension_semantics=sems, vmem_limit_bytes=int(limit))


def _nbytes(shape, dtype):
    return int(np.prod(shape)) * jnp.dtype(dtype).itemsize


def _gelu(x):
    return x * (0.5 * (1.0 + jnp.tanh(0.7978845608028654 * (x + 0.044715 * (x * x * x)))))


def _silu(x):
    return x * jax.nn.sigmoid(x)


_ACT = {"gelu": _gelu, "sigmoid": jax.nn.sigmoid, "none": lambda x: x}


def _rms(x):
    return x * lax.rsqrt(jnp.mean(x * x, axis=-1, keepdims=True) + NORM_EPS)


def _dot(a, b):
    return jnp.dot(a.astype(BF16), b.astype(BF16), preferred_element_type=F32)


def _dot_nt(a, b):
    return lax.dot_general(a, b, (((1,), (1,)), ((), ())), preferred_element_type=F32)


def _wspec(layer, k, tn, col_map):
    return pl.BlockSpec((None, k, tn), lambda *ids: (layer, 0, col_map(*ids)))


def _rmsnorm_kernel(x_ref, g_ref, o_ref):
    o_ref[...] = (_rms(x_ref[...]) * g_ref[...]).astype(o_ref.dtype)


def _rmsnorm(x, g, out_dtype, tm):
    m, d = x.shape
    return pl.pallas_call(
        _rmsnorm_kernel,
        grid=(m // tm,),
        in_specs=[pl.BlockSpec((tm, d), lambda i: (i, 0)), pl.BlockSpec((1, d), lambda i: (0, 0))],
        out_specs=pl.BlockSpec((tm, d), lambda i: (i, 0)),
        out_shape=jax.ShapeDtypeStruct((m, d), out_dtype),
        compiler_params=_params(("parallel",), _nbytes((tm, d), F32) + _nbytes((tm, d), out_dtype)),
        name="rmsnorm",
    )(x, g.reshape(1, d))


def _proj_act_kernel(x_ref, w_ref, o_ref, *, act):
    o_ref[...] = _ACT[act](_dot(x_ref[...], w_ref[...])).astype(o_ref.dtype)


def _proj_act(h, w, layer, col0, n_tiles, act, out_dtype, tm):
    m, k = h.shape
    win = _nbytes((tm, k), h.dtype) + _nbytes((k, TN), BF16) + _nbytes((tm, TN), out_dtype)
    return pl.pallas_call(
        functools.partial(_proj_act_kernel, act=act),
        grid=(m // tm, n_tiles),
        in_specs=[pl.BlockSpec((tm, k), lambda i, j: (i, 0)),
                  _wspec(layer, k, TN, lambda i, j: j + col0)],
        out_specs=pl.BlockSpec((tm, TN), lambda i, j: (i, j)),
        out_shape=jax.ShapeDtypeStruct((m, n_tiles * TN), out_dtype),
        compiler_params=_params(("parallel", "arbitrary"), win),
        name="proj_" + act,
    )(h, w)


def _q_kernel(x_ref, w_ref, g_ref, o_ref):
    acc = _dot(x_ref[...], w_ref[...])
    for hh in range(HEADS_PER_GROUP):
        a = acc[:, hh * HEAD_DIM:(hh + 1) * HEAD_DIM]
        o_ref[0, hh] = ((_rms(a) * g_ref[...]) * Q_SCALE).astype(o_ref.dtype)


def _q_proj(h, w, layer, g_q, batch, seq, tm):
    m, k = h.shape
    tpb = seq // tm
    win = _nbytes((tm, k), BF16) + _nbytes((k, TN), BF16) + _nbytes((tm, TN), BF16)
    return pl.pallas_call(
        _q_kernel,
        grid=(m // tm, N_GROUPS),
        in_specs=[pl.BlockSpec((tm, k), lambda i, j: (i, 0)),
                  _wspec(layer, k, TN, lambda i, j: j + COL_Q),
                  pl.BlockSpec((1, HEAD_DIM), lambda i, j: (0, 0))],
        out_specs=pl.BlockSpec((1, HEADS_PER_GROUP, tm, HEAD_DIM),
                               lambda i, j: (i // tpb, j, i % tpb, 0)),
        out_shape=jax.ShapeDtypeStruct((batch, N_ATTN_HEADS, seq, HEAD_DIM), BF16),
        compiler_params=_params(("parallel", "arbitrary"), win),
        name="q_proj",
    )(h, w, g_q.reshape(1, HEAD_DIM))


def _kv_kernel(x_ref, wk_ref, wv_ref, g_ref, kv_prev_ref, khm_ref, vhm_ref, kv_ref, *, rows):
    del kv_prev_ref
    x = x_ref[...]
    k = _dot(x, wk_ref[...])
    v = _dot(x, wv_ref[...])
    tm = x.shape[0]
    for hh in range(HEADS_PER_GROUP):
        sl = slice(hh * HEAD_DIM, (hh + 1) * HEAD_DIM)
        kn = _rms(k[:, sl]) * g_ref[...]
        vh = v[:, sl]
        khm_ref[0, hh] = kn.astype(BF16)
        vhm_ref[0, hh] = vh.astype(BF16)
        kv_ref[pl.ds(hh, rows, stride=KV_ROWS), :] = kn[tm - rows:]
        kv_ref[pl.ds(HEADS_PER_GROUP + hh, rows, stride=KV_ROWS), :] = vh[tm - rows:]


def _kv_proj(h, w, layer, depth, g_k, gi, kv_prev, batch, seq, tm):
    m, k = h.shape
    tpb = seq // tm
    keep = min(DILATION_GROUPS[gi][0], seq)
    rows = min(keep, tm)
    kv_tiles = keep // rows
    kv_map = lambda i: (layer, i // tpb, jnp.maximum(i % tpb - (tpb - kv_tiles), 0), 0)
    hm_spec = pl.BlockSpec((1, HEADS_PER_GROUP, tm, HEAD_DIM), lambda i: (i // tpb, 0, i % tpb, 0))
    hm_shape = jax.ShapeDtypeStruct((batch, HEADS_PER_GROUP, seq, HEAD_DIM), BF16)
    kv_shape = jax.ShapeDtypeStruct((depth, batch, keep * KV_ROWS, HEAD_DIM), F32)
    win = (_nbytes((tm, k), BF16) + _nbytes((k, TN), BF16) + 2 * _nbytes((tm, TN), BF16)
           + _nbytes((rows * KV_ROWS, HEAD_DIM), F32))
    if kv_prev is None:
        kv_prev = jnp.zeros(kv_shape.shape, F32)
    return pl.pallas_call(
        functools.partial(_kv_kernel, rows=rows),
        grid=(m // tm,),
        in_specs=[pl.BlockSpec((tm, k), lambda i: (i, 0)),
                  pl.BlockSpec((None, k, TN), lambda i: (layer, 0, COL_K + gi), pipeline_mode=pl.Buffered(1)),
                  pl.BlockSpec((None, k, TN), lambda i: (layer, 0, COL_V + gi), pipeline_mode=pl.Buffered(1)),
                  pl.BlockSpec((1, HEAD_DIM), lambda i: (0, 0)),
                  pl.BlockSpec(memory_space=pl.ANY)],
        out_specs=[hm_spec, hm_spec, pl.BlockSpec((None, None, rows * KV_ROWS, HEAD_DIM), kv_map)],
        out_shape=[hm_shape, hm_shape, kv_shape],
        input_output_aliases={4: 2},
        compiler_params=_params(("arbitrary",), win),
        name="kv_proj",
    )(h, w, w, g_k.reshape(1, HEAD_DIM), kv_prev)


ATTN_BATCH = 4


def _attn_blocks(qkv, bias):
    scores = [_dot_nt(q, k) + bias for q, k, _ in qkv]
    tops = [jnp.max(s, axis=-1, keepdims=True) for s in scores]
    probs = [jnp.exp(s - m).astype(BF16) for s, m in zip(scores, tops)]
    nds = [jnp.dot(e, jnp.concatenate([v, jnp.ones_like(v)], axis=1), preferred_element_type=F32)
           for e, (_, _, v) in zip(probs, qkv)]
    return [(nd[:, :HEAD_DIM] / nd[:, HEAD_DIM:], m + jnp.log(nd[:, HEAD_DIM:])) for nd, m in zip(nds, tops)]


def _attn_kernel(q0_ref, q1_ref, q2_ref, k0_ref, k1_ref, k2_ref, v0_ref, v1_ref, v2_ref, bias_ref, o_ref,
                 q1f, q2f, k1f, k2f, v1f, v2f, o0_sc, o1_sc, o2_sc, l0_sc, l1_sc, l2_sc, *, seq):
    for src, dst in ((q1_ref, q1f), (q2_ref, q2f), (k1_ref, k1f), (k2_ref, k2f), (v1_ref, v1f), (v2_ref, v2f)):
        dst[...] = src[0, 0].astype(F32)

    def run_group(gi, load, o_sc, l_sc, unroll):
        dil = DILATION_GROUPS[gi][1]
        nblk = seq // dil // BLK

        def rows(r, s0, n):
            if dil == 1:
                return pl.ds(s0 if isinstance(s0, int) else pl.multiple_of(s0, BLK), n)
            return pl.ds(s0 * dil + r, n, stride=dil)

        def blocks(todo, first):
            back = 0 if first else BLK
            bias = bias_ref[gi, 0, :, BLK:2 * BLK] if first else bias_ref[gi, 0]
            qkv = [load(rows(r, s0, BLK), rows(r, s0 - back, BLK + back)) for r, s0 in todo]
            for (r, s0), (o, lse) in zip(todo, _attn_blocks(qkv, bias)):
                o_sc[rows(r, s0, BLK), :] = o
                l_sc[rows(r, s0, BLK), :] = lse

        for r0 in range(0, dil, ATTN_BATCH):
            blocks([(r, 0) for r in range(r0, min(dil, r0 + ATTN_BATCH))], True)

        def body(it, carry):
            blocks([(r, (1 + it * unroll + u) * BLK) for u in range(unroll) for r in range(dil)], False)
            return carry

        if nblk > 1:
            lax.fori_loop(0, (nblk - 1) // unroll, body, 0)

    def bf16_load(q_ref, k_ref, v_ref):
        return lambda qr, kr: (q_ref[0, 0, qr, :], k_ref[0, 0, kr, :], v_ref[0, 0, kr, :])

    def f32_load(qf, kf, vf):
        return lambda qr, kr: (qf[qr, :].astype(BF16), kf[kr, :].astype(BF16), vf[kr, :].astype(BF16))

    run_group(0, bf16_load(q0_ref, k0_ref, v0_ref), o0_sc, l0_sc, 5)
    run_group(1, f32_load(q1f, k1f, v1f), o1_sc, l1_sc, 1)
    run_group(2, f32_load(q2f, k2f, v2f), o2_sc, l2_sc, 1)

    def combine(c, carry):
        rows = pl.ds(pl.multiple_of(c * BLK, BLK), BLK)
        l0, l1, l2 = l0_sc[rows, :], l1_sc[rows, :], l2_sc[rows, :]
        top = jnp.maximum(jnp.maximum(l0, l1), l2)
        w0, w1, w2 = jnp.exp(l0 - top), jnp.exp(l1 - top), jnp.exp(l2 - top)
        mixed = (w0 * o0_sc[rows, :] + w1 * o1_sc[rows, :] + w2 * o2_sc[rows, :]) / (w0 + w1 + w2)
        o_ref[0, rows, :] = mixed.astype(o_ref.dtype)
        return carry

    lax.fori_loop(0, seq // BLK, combine, 0)


def _attention(q_hm, ks, vs, bias_tab, batch, seq):
    slab = (1, 1, seq, HEAD_DIM)
    specs = [pl.BlockSpec(slab, lambda b, h, gi=gi: (b, gi * HEADS_PER_GROUP + h, 0, 0))
             for gi in range(N_GROUPS)]
    specs += [pl.BlockSpec(slab, lambda b, h: (b, h, 0, 0))] * (2 * N_GROUPS)
    specs.append(pl.BlockSpec((N_GROUPS, 1, BLK, 2 * BLK), lambda b, h: (0, h, 0, 0)))
    win = (10 * _nbytes((seq, HEAD_DIM), BF16) + _nbytes((N_GROUPS, BLK, 2 * BLK), F32)
           + 6 * _nbytes((seq, HEAD_DIM), F32))
    return pl.pallas_call(
        functools.partial(_attn_kernel, seq=seq),
        grid=(batch, HEADS_PER_GROUP),
        in_specs=specs,
        out_specs=pl.BlockSpec((1, seq, HEAD_DIM), lambda b, h: (b, 0, h)),
        out_shape=jax.ShapeDtypeStruct((batch, seq, ATTN_OUT), BF16),
        scratch_shapes=[pltpu.VMEM((seq, HEAD_DIM), F32)] * 12,
        compiler_params=_params(("parallel", "arbitrary"), win),
        name="dilated_attention",
    )(q_hm, q_hm, q_hm, *ks, *vs, bias_tab)


def _sgu_kernel(u_ref, gv_ref, g_ref, w_ref, bt_ref, o_ref, vs_sc):
    gv = gv_ref[...].astype(F32)
    vs_sc[...] = (_rms(gv) * g_ref[...]).astype(BF16)
    tm = gv.shape[0]
    row = lax.broadcasted_iota(jnp.int32, (BLK, BLK), 0)
    col = lax.broadcasted_iota(jnp.int32, (BLK, BLK), 1)
    tril = (row >= col).astype(F32)
    for g in range(GM_GROUPS):
        cols = slice(g * BLK, (g + 1) * BLK)
        wg = (w_ref[g] * tril).astype(BF16)
        bg = bt_ref[:, g:g + 1]
        for c in range(tm // BLK):
            rows = slice(c * BLK, (c + 1) * BLK)
            mix = _dot(wg, vs_sc[rows, cols]) + bg
            o_ref[rows, cols] = (u_ref[rows, cols].astype(F32) * mix).astype(o_ref.dtype)


def _sgu(uv, g_sgu, w_sgu, b_sgu, tm):
    m, c = uv.shape[0], GM_WIDTH
    win = 3 * _nbytes((tm, c), BF16) + _nbytes((GM_GROUPS, BLK, BLK), F32)
    return pl.pallas_call(
        _sgu_kernel,
        grid=(m // tm,),
        in_specs=[pl.BlockSpec((tm, c), lambda i: (i, 0)),
                  pl.BlockSpec((tm, c), lambda i: (i, 1)),
                  pl.BlockSpec((1, c), lambda i: (0, 0)),
                  pl.BlockSpec((GM_GROUPS, BLK, BLK), lambda i: (0, 0, 0)),
                  pl.BlockSpec((BLK, GM_GROUPS), lambda i: (0, 0))],
        out_specs=pl.BlockSpec((tm, c), lambda i: (i, 0)),
        out_shape=jax.ShapeDtypeStruct((m, c), BF16),
        scratch_shapes=[pltpu.VMEM((tm, c), BF16)],
        compiler_params=_params(("parallel",), win),
        name="spatial_gating",
    )(uv, uv, g_sgu.reshape(1, c), w_sgu, b_sgu.T)


def _merge_kernel(a_ref, s_ref, wa_ref, wb_ref, ga_ref, gb_ref, o_ref):
    a = _dot(a_ref[...], wa_ref[...])
    b = _dot(s_ref[...], wb_ref[...])
    o_ref[...] = (ga_ref[...].astype(F32) * a + gb_ref[...].astype(F32) * b).astype(o_ref.dtype)


def _merge(attn, sgu, wa, wb, layer, ga, gb, gb_col0, out_dtype, tm):
    m = attn.shape[0]
    win = (_nbytes((tm, ATTN_OUT), attn.dtype) + _nbytes((tm, GM_WIDTH), sgu.dtype)
           + _nbytes((ATTN_OUT + GM_WIDTH, TN), BF16) + 2 * _nbytes((tm, TN), ga.dtype)
           + _nbytes((tm, TN), out_dtype))
    row = lambda width: pl.BlockSpec((tm, width), lambda i, j: (i, 0))
    tile = pl.BlockSpec((tm, TN), lambda i, j: (i, j))
    return pl.pallas_call(
        _merge_kernel,
        grid=(m // tm, D_MODEL // TN),
        in_specs=[row(ATTN_OUT), row(GM_WIDTH),
                  _wspec(layer, ATTN_OUT, TN, lambda i, j: j),
                  _wspec(layer, GM_WIDTH, TN, lambda i, j: j), tile,
                  pl.BlockSpec((tm, TN), lambda i, j: (i, j + gb_col0))],
        out_specs=tile,
        out_shape=jax.ShapeDtypeStruct((m, D_MODEL), out_dtype),
        compiler_params=_params(("parallel", "arbitrary"), win),
        name="branch_merge",
    )(attn, sgu, wa, wb, ga, gb)


def _mm_res_kernel(x_ref, w_ref, r_ref, o_ref, *, k_rows):
    w = w_ref[...]
    tk = w.shape[0]
    if k_rows % tk:
        row = pl.program_id(2) * tk + lax.broadcasted_iota(jnp.int32, w.shape, 0)
        w = jnp.where(row < k_rows, w, jnp.zeros_like(w))
    part = _dot(x_ref[...], w)

    @pl.when(pl.program_id(2) == 0)
    def _():
        o_ref[...] = r_ref[...] + part

    @pl.when(pl.program_id(2) > 0)
    def _():
        o_ref[...] += part


def _mm_res(x, w, layer, res, tm, tk):
    m, k = x.shape
    k_rows, n = w.shape[1:]
    win = _nbytes((tm, tk), x.dtype) + _nbytes((tk, TN), BF16) + 2 * _nbytes((tm, TN), F32)
    return pl.pallas_call(
        functools.partial(_mm_res_kernel, k_rows=k_rows),
        grid=(m // tm, n // TN, k // tk),
        in_specs=[pl.BlockSpec((tm, tk), lambda i, j, kk: (i, kk)),
                  pl.BlockSpec((None, tk, TN), lambda i, j, kk: (layer, kk, j)),
                  pl.BlockSpec((tm, TN), lambda i, j, kk: (i, j))],
        out_specs=pl.BlockSpec((tm, TN), lambda i, j, kk: (i, j)),
        out_shape=jax.ShapeDtypeStruct((m, n), F32),
        compiler_params=_params(("parallel", "parallel", "arbitrary"), win),
        name="matmul_residual",
    )(x, w, res)


FFN_TN = 512


def _glu_kernel(x_ref, wg_ref, wu_ref, o_ref, *, n_cols):
    x = x_ref[...]
    y = _silu(_dot(x, wg_ref[...])) * _dot(x, wu_ref[...])
    col = pl.program_id(1) * y.shape[1] + lax.broadcasted_iota(jnp.int32, y.shape, 1)
    o_ref[...] = jnp.where(col < n_cols, y, 0.0).astype(o_ref.dtype)


def _glu(h, wg, wu, layer, n, out_dtype, tm):
    m, k = h.shape
    win = _nbytes((tm, k), h.dtype) + 2 * _nbytes((k, FFN_TN), BF16) + _nbytes((tm, FFN_TN), out_dtype)
    wspec = _wspec(layer, k, FFN_TN, lambda i, j: j)
    return pl.pallas_call(
        functools.partial(_glu_kernel, n_cols=wg.shape[2]),
        grid=(m // tm, n // FFN_TN),
        in_specs=[pl.BlockSpec((tm, k), lambda i, j: (i, 0)), wspec, wspec],
        out_specs=pl.BlockSpec((tm, FFN_TN), lambda i, j: (i, j)),
        out_shape=jax.ShapeDtypeStruct((m, n), out_dtype),
        compiler_params=_params(("parallel", "arbitrary"), win),
        name="swiglu_up",
    )(h, wg, wu)


def _ple_kernel(h_ref, p_ref, wg_ref, wp_ref, r_ref, o_ref):
    gate = jax.nn.sigmoid(_dot(h_ref[...], wg_ref[...]))
    o_ref[...] = r_ref[...] + gate * _dot(p_ref[...], wp_ref[...])


def _ple(h, p, wg, wp, layer, res, tm):
    m, k = h.shape
    tn = FFN_TN
    win = (_nbytes((tm, k), h.dtype) + _nbytes((tm, PLE_DIM), F32) + _nbytes((k + PLE_DIM, tn), BF16)
           + 2 * _nbytes((tm, tn), F32))
    tile = pl.BlockSpec((tm, tn), lambda i, j: (i, j))
    return pl.pallas_call(
        _ple_kernel,
        grid=(m // tm, D_MODEL // tn),
        in_specs=[pl.BlockSpec((tm, k), lambda i, j: (i, 0)),
                  pl.BlockSpec((None, tm, PLE_DIM), lambda i, j: (layer, i, 0)),
                  _wspec(layer, k, tn, lambda i, j: j),
                  _wspec(layer, PLE_DIM, tn, lambda i, j: j), tile],
        out_specs=tile,
        out_shape=jax.ShapeDtypeStruct((m, D_MODEL), F32),
        compiler_params=_params(("parallel", "arbitrary"), win),
        name="ple_gate",
    )(h, p, wg, wp, res)


def _sample_mix_kernel(zh_ref, zr_ref, c0_ref, c1_ref, c2_ref, gq_ref, gk_ref, gsgu_ref,
                       w00_ref, b0_ref, bias_ref, bias0_ref,
                       kvn_ref, attn_ref, sguv_ref, sguo_ref, ga_ref, gb_ref, qn_sc, kn_sc):
    nh, hpg = N_ATTN_HEADS, HEADS_PER_GROUP
    qn_sc[...] = (_rms(zh_ref[0, 0:nh, :]) * gq_ref[...]) * Q_SCALE
    kn_sc[...] = _rms(zh_ref[0, nh:2 * nh, :]) * gk_ref[...]

    outs, lses = [], []
    for gi, c_ref in enumerate((c0_ref, c1_ref, c2_ref)):
        hs = slice(gi * hpg, (gi + 1) * hpg)
        q, k_new = qn_sc[hs, :], kn_sc[hs, :]
        v_new = zh_ref[0, 2 * nh + gi * hpg:2 * nh + (gi + 1) * hpg, :]
        kvn_ref[0, gi, 0] = k_new
        kvn_ref[0, gi, 1] = v_new
        kc = c_ref[0, 0, :, 0:hpg, :]
        vc = c_ref[0, 0, :, hpg:2 * hpg, :]
        s = jnp.sum(kc * q[None], axis=-1, keepdims=True) + bias_ref[gi]
        s_new = jnp.sum(q * k_new, axis=-1, keepdims=True) + bias0_ref[gi]
        m = jnp.maximum(jnp.max(s, axis=0), s_new)
        e = jnp.exp(s - m[None])
        e_new = jnp.exp(s_new - m)
        den = jnp.sum(e, axis=0) + e_new
        outs.append((jnp.sum(e * vc, axis=0) + e_new * v_new) / den)
        lses.append(m + jnp.log(den))
    top = jnp.maximum(jnp.maximum(lses[0], lses[1]), lses[2])
    wts = [jnp.exp(l - top) for l in lses]
    attn_ref[0] = (wts[0] * outs[0] + wts[1] * outs[1] + wts[2] * outs[2]) / (wts[0] + wts[1] + wts[2])

    c = GM_WIDTH
    u = _gelu(zr_ref[0, :, 0:c])
    vs = _rms(_gelu(zr_ref[0, :, c:2 * c])) * gsgu_ref[...]
    sguv_ref[0] = vs
    sguo_ref[0] = u * (w00_ref[...] * vs + b0_ref[...])
    ga_ref[0] = jax.nn.sigmoid(zr_ref[0, :, 2 * c:2 * c + D_MODEL])
    gb_ref[0] = jax.nn.sigmoid(zr_ref[0, :, 2 * c + D_MODEL:2 * c + 2 * D_MODEL])


def _sample_mix(z, caches, layer, g_q, g_k, g_sgu, w_sgu, b_sgu, bias_s, bias0_s):
    nb = z.shape[0]
    hpg = HEADS_PER_GROUP
    zh = z[:, :3 * ATTN_WIDTH].reshape(nb, 3 * N_ATTN_HEADS, HEAD_DIM)
    rest = 2 * GM_WIDTH + 2 * D_MODEL
    zr = z[:, 3 * ATTN_WIDTH:].reshape(nb, 1, rest)
    w00 = jnp.repeat(w_sgu[:, 0, 0], BLK).reshape(1, GM_WIDTH)
    b0 = jnp.repeat(b_sgu[:, 0], BLK).reshape(1, GM_WIDTH)
    cache_views, cache_specs = [], []
    for c, (_, dil) in zip(caches, DILATION_GROUPS):
        depth, _, length = c.shape[:3]
        cache_views.append(c.reshape(depth, nb, length // dil, dil * KV_ROWS, HEAD_DIM))
        cache_specs.append(pl.BlockSpec((1, 1, BLK, KV_ROWS, HEAD_DIM), lambda b: (layer, b, 0, 0, 0)))
    vec = lambda width: pl.BlockSpec((1, width), lambda b: (0, 0))
    per_b = lambda width: pl.BlockSpec((1, 1, width), lambda b: (b, 0, 0))
    win = (3 * _nbytes((BLK, KV_ROWS, HEAD_DIM), F32) + _nbytes((1, 4 * rest), F32)
           + _nbytes((N_GROUPS, BLK + 1, hpg, HEAD_DIM), F32))
    return pl.pallas_call(
        _sample_mix_kernel,
        grid=(nb,),
        in_specs=[pl.BlockSpec((1, 3 * N_ATTN_HEADS, HEAD_DIM), lambda b: (b, 0, 0)), per_b(rest),
                  *cache_specs, vec(HEAD_DIM), vec(HEAD_DIM), vec(GM_WIDTH), vec(GM_WIDTH),
                  vec(GM_WIDTH),
                  pl.BlockSpec((N_GROUPS, BLK, hpg, HEAD_DIM), lambda b: (0, 0, 0, 0)),
                  pl.BlockSpec((N_GROUPS, hpg, HEAD_DIM), lambda b: (0, 0, 0))],
        out_specs=[pl.BlockSpec((1, N_GROUPS, 2, hpg, HEAD_DIM), lambda b: (b, 0, 0, 0, 0)),
                   pl.BlockSpec((1, hpg, HEAD_DIM), lambda b: (b, 0, 0)),
                   per_b(GM_WIDTH), per_b(GM_WIDTH), per_b(D_MODEL), per_b(D_MODEL)],
        out_shape=[jax.ShapeDtypeStruct((nb, N_GROUPS, 2, hpg, HEAD_DIM), F32),
                   jax.ShapeDtypeStruct((nb, hpg, HEAD_DIM), F32),
                   jax.ShapeDtypeStruct((nb, 1, GM_WIDTH), F32),
                   jax.ShapeDtypeStruct((nb, 1, GM_WIDTH), F32),
                   jax.ShapeDtypeStruct((nb, 1, D_MODEL), F32),
                   jax.ShapeDtypeStruct((nb, 1, D_MODEL), F32)],
        scratch_shapes=[pltpu.VMEM((N_ATTN_HEADS, HEAD_DIM), F32)] * 2,
        compiler_params=_params(("parallel",), win),
        name="decode_mixers",
    )(zh, zr, *cache_views, g_q.reshape(1, HEAD_DIM), g_k.reshape(1, HEAD_DIM),
      g_sgu.reshape(1, GM_WIDTH), w00, b0, bias_s, bias0_s)


ROLL_TOKENS = 512


def _roll_kernel(chunk_ref, next_ref, new_ref, o_ref):
    kept = chunk_ref.shape[0] - KV_ROWS
    o_ref[0:kept, :] = chunk_ref[KV_ROWS:, :]
    last = pl.program_id(2) == pl.num_programs(2) - 1

    @pl.when(last)
    def _():
        o_ref[kept:, :] = new_ref[...]

    @pl.when(jnp.logical_not(last))
    def _():
        o_ref[kept:, :] = next_ref[...]


def _roll_cache(cache, new_rows):
    depth, nb, length = cache.shape[:3]
    tokens = min(length, ROLL_TOKENS)
    rows = tokens * KV_ROWS
    flat = cache.reshape(depth, nb, length * KV_ROWS, HEAD_DIM)
    token = (None, None, KV_ROWS, HEAD_DIM)
    out = pl.pallas_call(
        _roll_kernel,
        grid=(depth, nb, length // tokens),
        in_specs=[pl.BlockSpec((None, None, rows, HEAD_DIM), lambda d, b, c: (d, b, c, 0)),
                  pl.BlockSpec(token, lambda d, b, c: (d, b, jnp.minimum((c + 1) * tokens, length - 1), 0)),
                  pl.BlockSpec(token, lambda d, b, c: (d, b, 0, 0))],
        out_specs=pl.BlockSpec((None, None, rows, HEAD_DIM), lambda d, b, c: (d, b, c, 0)),
        out_shape=jax.ShapeDtypeStruct(flat.shape, cache.dtype),
        compiler_params=_params(("parallel", "parallel", "arbitrary"), 2 * _nbytes((rows, HEAD_DIM), F32)),
        name="roll_cache",
    )(flat, flat, new_rows.reshape(depth, nb, KV_ROWS, HEAD_DIM))
    return out.reshape(cache.shape)


def _t5_bucket(dist):
    dist = np.asarray(dist)
    max_exact = NUM_BUCKETS // 2
    large = max_exact + (np.log(np.maximum(dist, 1) / max_exact) / np.log(REL_MAX_DIST / max_exact)
                         * (NUM_BUCKETS - max_exact)).astype(np.int32)
    large = np.minimum(large, NUM_BUCKETS - 1)
    return np.where(dist < max_exact, dist, large).astype(np.int32)


def _bias_tables(rel_bias):
    hpg = HEADS_PER_GROUP
    prompt, dec, dec0 = [], [], []
    for gi, (_, dil) in enumerate(DILATION_GROUPS):
        heads = rel_bias[:, gi * hpg:(gi + 1) * hpg].astype(F32)
        by_off = jnp.take(heads, _t5_bucket(np.arange(BLK + 1) * dil), axis=0).T
        w = jnp.concatenate([by_off[:, ::-1], jnp.full((hpg, BLK), NEG_INF, F32)], axis=1)
        tab = jnp.tile(w, (1, BLK))[:, :BLK * 2 * BLK].reshape(hpg, BLK, 2 * BLK)
        prompt.append(tab)
        dec.append(jnp.broadcast_to(by_off[:, :0:-1].T[:, :, None], (BLK, hpg, HEAD_DIM)))
        dec0.append(jnp.broadcast_to(by_off[:, 0:1], (hpg, HEAD_DIM)))
    return jnp.stack(prompt), jnp.stack(dec), jnp.stack(dec0)


def _cast_weights(w_in, w_attn_br, w_sgu_br, w_out, w_ffn_gate, w_ffn_up, w_ffn_down,
                  w_ple_gate, w_ple_proj):
    return dict(
        w_in=w_in.astype(BF16), w_attn_br=w_attn_br.astype(BF16),
        w_sgu_br=w_sgu_br.astype(BF16), w_out=w_out.astype(BF16),
        w_gate=w_ffn_gate.astype(BF16), w_up=w_ffn_up.astype(BF16), w_down=w_ffn_down.astype(BF16),
        w_ple_gate=w_ple_gate.astype(BF16), w_ple_proj=w_ple_proj.astype(BF16))


def _tail(x, p, merged_in, wts, layer, g_ffn, g_ple, tm, act_dtype):
    attn, sgu, ga, gb, gb_col0 = merged_in
    merged = _merge(attn, sgu, wts["w_attn_br"], wts["w_sgu_br"], layer, ga, gb, gb_col0, act_dtype, tm)
    x = _mm_res(merged, wts["w_out"], layer, x, tm, D_MODEL)
    h2 = _rmsnorm(x, g_ffn, act_dtype, min(tm, 256))
    ff = _glu(h2, wts["w_gate"], wts["w_up"], layer, D_FF_PAD, act_dtype, tm)
    x = _mm_res(ff, wts["w_down"], layer, x, tm, D_FF_PAD // 4)
    h3 = _rmsnorm(x, g_ple, act_dtype, min(tm, 256))
    return _ple(h3, p, wts["w_ple_gate"], wts["w_ple_proj"], layer, x, tm)


def _prompt_layer(x, p, wts, layer, depth, kv_prev, bias_tab, g_mix, g_q, g_k, g_sgu, w_sgu, b_sgu,
                  g_ffn, g_ple, batch, seq):
    tm = 1024
    h = _rmsnorm(x, g_mix, BF16, 256)
    q_hm = _q_proj(h, wts["w_in"], layer, g_q, batch, seq, tm)
    ks, vs, kvs = [], [], []
    for gi in range(N_GROUPS):
        k_hm, v_hm, kv = _kv_proj(h, wts["w_in"], layer, depth, g_k, gi, kv_prev[gi], batch, seq, 512)
        ks.append(k_hm)
        vs.append(v_hm)
        kvs.append(kv)
    attn = _attention(q_hm, ks, vs, bias_tab, batch, seq).reshape(batch * seq, ATTN_OUT)
    uv = _proj_act(h, wts["w_in"], layer, COL_U, 4, "gelu", BF16, tm)
    gates = _proj_act(h, wts["w_in"], layer, COL_GATE, 8, "sigmoid", BF16, tm)
    sgu = _sgu(uv, g_sgu, w_sgu, b_sgu, 512)
    x = _tail(x, p, (attn, sgu, gates, gates, D_MODEL // TN), wts, layer, g_ffn, g_ple, tm, BF16)
    return x, kvs


def _sample_layer(x, p, caches, wts, layer, bias_s, bias0_s, g_mix, g_q, g_k, g_sgu, w_sgu, b_sgu,
                  g_ffn, g_ple):
    nb = x.shape[0]
    h = _rmsnorm(x, g_mix, F32, nb)
    z = _proj_act(h, wts["w_in"], layer, 0, N_IN_TILES, "none", F32, nb)
    kvn, attn, sguv, sguo, ga, gb = _sample_mix(z, caches, layer, g_q, g_k, g_sgu, w_sgu, b_sgu,
                                                bias_s, bias0_s)
    mixers = (attn.reshape(nb, ATTN_OUT), sguo.reshape(nb, GM_WIDTH),
              ga.reshape(nb, D_MODEL), gb.reshape(nb, D_MODEL), 0)
    x = _tail(x, p, mixers, wts, layer, g_ffn, g_ple, nb, F32)
    return x, kvn, sguv


def kernel(x_prompt, x_sample, cache_kv_w128, cache_kv_w512, cache_kv_w2048, p_prompt, p_sample,
           rel_bias, g_mix, w_in, g_q, g_k, g_sgu, w_sgu, b_sgu, w_attn_br, w_sgu_br, w_out, g_ffn,
           w_ffn_gate, w_ffn_up, w_ffn_down, g_ple, w_ple_gate, w_ple_proj):
    batch, seq, _ = x_prompt.shape
    nb = x_sample.shape[0]
    depth = w_in.shape[0]
    caches = (cache_kv_w128, cache_kv_w512, cache_kv_w2048)
    bias_tab, bias_s, bias0_s = _bias_tables(rel_bias)
    wts = _cast_weights(w_in, w_attn_br, w_sgu_br, w_out, w_ffn_gate, w_ffn_up, w_ffn_down,
                        w_ple_gate, w_ple_proj)

    xp = x_prompt.reshape(batch * seq, D_MODEL)
    xs = x_sample.reshape(nb, D_MODEL)
    pp = p_prompt.reshape(depth, batch * seq, PLE_DIM)
    ps = p_sample.reshape(depth, nb, PLE_DIM)
    kv_p = [None] * N_GROUPS
    kv_new = [[] for _ in range(N_GROUPS)]
    sgu_v = []
    for i in range(depth):
        norms = (g_mix[i], g_q[i], g_k[i], g_sgu[i], w_sgu[i], b_sgu[i], g_ffn[i], g_ple[i])
        xp, kv_p = _prompt_layer(xp, pp, wts, i, depth, kv_p, bias_tab, *norms, batch, seq)
        xs, kvn, sguv = _sample_layer(xs, ps, caches, wts, i, bias_s, bias0_s, *norms)
        for gi in range(N_GROUPS):
            kv_new[gi].append(kvn[:, gi])
        sgu_v.append(sguv)

    kv_p = [kv.reshape(depth, batch, -1, 2, HEADS_PER_GROUP, HEAD_DIM) for kv in kv_p]
    kv_s = [_roll_cache(c, jnp.stack(kv_new[gi], axis=0)) for gi, c in enumerate(caches)]
    return (xp.reshape(batch, seq, D_MODEL), xs.reshape(nb, 1, D_MODEL), *kv_p, *kv_s,
            jnp.stack(sgu_v, axis=0))
```

```python
import functools

import numpy as np
import jax
import jax.numpy as jnp
from jax import lax
from jax.experimental import pallas as pl
from jax.experimental.pallas import tpu as pltpu

D_MODEL = 4096
HEAD_DIM = 128
DILATION_GROUPS = ((128, 1), (512, 4), (2048, 16))
N_GROUPS = 3
HEADS_PER_GROUP = 8
N_ATTN_HEADS = 24
ATTN_WIDTH = 3072
ATTN_OUT = 1024
BLK = 128
KV_ROWS = 2 * HEADS_PER_GROUP
GM_WIDTH = 2048
GM_GROUPS = 16
PLE_DIM = 256
NUM_BUCKETS = 32
REL_MAX_DIST = 2048
NORM_EPS = 1e-6
NEG_INF = -1e30
Q_SCALE = HEAD_DIM ** -0.5

TN = 1024
COL_U, COL_GATE = 9, 13

V7X_VMEM_REQUEST_CAP = 60000 * 1024
COMPILER_TEMP_BYTES = 10 * 1024 * 1024

F32 = jnp.float32
BF16 = jnp.bfloat16


def _params(sems, window_bytes, scratch_bytes=0):
    limit = min(V7X_VMEM_REQUEST_CAP, 2 * window_bytes + scratch_bytes + COMPILER_TEMP_BYTES)
    return pltpu.CompilerParams(dimension_semantics=sems, vmem_limit_bytes=int(limit))


def _nbytes(shape, dtype):
    return int(np.prod(shape)) * jnp.dtype(dtype).itemsize


def _gelu(x):
    return x * (0.5 * (1.0 + jnp.tanh(0.7978845608028654 * (x + 0.044715 * (x * x * x)))))


def _silu(x):
    return x * jax.nn.sigmoid(x)


_ACT = {"gelu": _gelu, "sigmoid": jax.nn.sigmoid, "none": lambda x: x}


def _rms(x):
    return x * lax.rsqrt(jnp.mean(x * x, axis=-1, keepdims=True) + NORM_EPS)


def _dot(a, b):
    return jnp.dot(a.astype(BF16), b.astype(BF16), preferred_element_type=F32)


def _dot_nt(a, b):
    return lax.dot_general(a, b, (((1,), (1,)), ((), ())), preferred_element_type=F32)


def _wspec(layer, k, tn, col_map):
    return pl.BlockSpec((None, k, tn), lambda *ids: (layer, 0, col_map(*ids)))


def _rmsnorm_kernel(x_ref, g_ref, o_ref):
    o_ref[...] = (_rms(x_ref[...]) * g_ref[...]).astype(o_ref.dtype)


def _rmsnorm(x, g, out_dtype, tm):
    m, d = x.shape
    return pl.pallas_call(
        _rmsnorm_kernel,
        grid=(m // tm,),
        in_specs=[pl.BlockSpec((tm, d), lambda i: (i, 0)), pl.BlockSpec((1, d), lambda i: (0, 0))],
        out_specs=pl.BlockSpec((tm, d), lambda i: (i, 0)),
        out_shape=jax.ShapeDtypeStruct((m, d), out_dtype),
        compiler_params=_params(("parallel",), _nbytes((tm, d), F32) + _nbytes((tm, d), out_dtype)),
        name="rmsnorm",
    )(x, g.reshape(1, d))


def _proj_rows_kernel(x_ref, w_ref, o_ref):
    o_ref[...] = _dot(x_ref[...], w_ref[...])


def _proj_rows(h, w, layer, n_tiles):
    m, k = h.shape
    win = _nbytes((m, k), h.dtype) + _nbytes((k, TN), BF16) + _nbytes((m, TN), F32)
    return pl.pallas_call(
        _proj_rows_kernel,
        grid=(n_tiles,),
        in_specs=[pl.BlockSpec((m, k), lambda j: (0, 0)), _wspec(layer, k, TN, lambda j: j)],
        out_specs=pl.BlockSpec((m, TN), lambda j: (0, j)),
        out_shape=jax.ShapeDtypeStruct((m, n_tiles * TN), F32),
        compiler_params=_params(("parallel",), win),
        name="proj_rows",
    )(h, w)


PTN = 512


def _resident_cast(w_refs, wb_refs):
    for w_ref, wb_ref in zip(w_refs, wb_refs):
        wb_ref[...] = w_ref[...].astype(BF16)


def _proj_act_kernel(x_ref, xs_ref, w_ref, o_ref, os_ref, wb, *, act):
    @pl.when(pl.program_id(1) == 0)
    def _():
        _resident_cast([w_ref], [wb])
        os_ref[...] = _dot(xs_ref[...], wb[...])

    o_ref[...] = _ACT[act](_dot(x_ref[...], wb[...])).astype(o_ref.dtype)


def _proj_act(h, hs, w, layer, col0, n_tiles, act, tm):
    (m, k), nb = h.shape, hs.shape[0]
    win = _nbytes((tm, k), BF16) + _nbytes((k, PTN), F32) + _nbytes((tm, PTN), BF16)
    return pl.pallas_call(
        functools.partial(_proj_act_kernel, act=act),
        grid=(n_tiles, m // tm),
        in_specs=[pl.BlockSpec((tm, k), lambda j, i: (i, 0)),
                  pl.BlockSpec((nb, k), lambda j, i: (0, 0)),
                  _wspec(layer, k, PTN, lambda j, i: j + col0)],
        out_specs=[pl.BlockSpec((tm, PTN), lambda j, i: (i, j)),
                   pl.BlockSpec((nb, PTN), lambda j, i: (0, j))],
        out_shape=[jax.ShapeDtypeStruct((m, n_tiles * PTN), BF16),
                   jax.ShapeDtypeStruct((nb, n_tiles * PTN), F32)],
        scratch_shapes=[pltpu.VMEM((k, PTN), BF16)],
        compiler_params=_params(("arbitrary", "arbitrary"), win, _nbytes((k, PTN), BF16)),
        name="proj_" + act,
    )(h, hs, w)


def _q_kernel(x_ref, xs_ref, w_ref, g_ref, o_ref, os_ref, wb):
    @pl.when(pl.program_id(1) == 0)
    def _():
        _resident_cast([w_ref], [wb])
        os_ref[...] = _dot(xs_ref[...], wb[...])

    acc = _dot(x_ref[...], wb[...])
    for hh in range(PTN // HEAD_DIM):
        a = acc[:, hh * HEAD_DIM:(hh + 1) * HEAD_DIM]
        o_ref[0, hh] = ((_rms(a) * g_ref[...]) * Q_SCALE).astype(o_ref.dtype)


def _q_proj(h, hs, w, layer, g_q, batch, seq, tm):
    (m, k), nb = h.shape, hs.shape[0]
    tpb = seq // tm
    heads = PTN // HEAD_DIM
    win = _nbytes((tm, k), BF16) + _nbytes((k, PTN), F32) + _nbytes((tm, PTN), BF16)
    return pl.pallas_call(
        _q_kernel,
        grid=(ATTN_WIDTH // PTN, m // tm),
        in_specs=[pl.BlockSpec((tm, k), lambda j, i: (i, 0)),
                  pl.BlockSpec((nb, k), lambda j, i: (0, 0)),
                  _wspec(layer, k, PTN, lambda j, i: j),
                  pl.BlockSpec((1, HEAD_DIM), lambda j, i: (0, 0))],
        out_specs=[pl.BlockSpec((1, heads, tm, HEAD_DIM), lambda j, i: (i // tpb, j, i % tpb, 0)),
                   pl.BlockSpec((nb, PTN), lambda j, i: (0, j))],
        out_shape=[jax.ShapeDtypeStruct((batch, N_ATTN_HEADS, seq, HEAD_DIM), BF16),
                   jax.ShapeDtypeStruct((nb, ATTN_WIDTH), F32)],
        scratch_shapes=[pltpu.VMEM((k, PTN), BF16)],
        compiler_params=_params(("arbitrary", "arbitrary"), win, _nbytes((k, PTN), BF16)),
        name="q_proj",
    )(h, hs, w, g_q.reshape(1, HEAD_DIM))


def _kv_kernel(x_ref, wk_ref, wv_ref, g_ref, kv_prev_ref, khm_ref, vhm_ref, kv_ref, *, rows):
    del kv_prev_ref
    x = x_ref[...]
    k = _dot(x, wk_ref[...])
    v = _dot(x, wv_ref[...])
    tm = x.shape[0]
    for hh in range(HEADS_PER_GROUP):
        sl = slice(hh * HEAD_DIM, (hh + 1) * HEAD_DIM)
        kn = _rms(k[:, sl]) * g_ref[...]
        vh = v[:, sl]
        khm_ref[0, hh] = kn.astype(BF16)
        vhm_ref[0, hh] = vh.astype(BF16)
        kv_ref[pl.ds(hh, rows, stride=KV_ROWS), :] = kn[tm - rows:]
        kv_ref[pl.ds(HEADS_PER_GROUP + hh, rows, stride=KV_ROWS), :] = vh[tm - rows:]


def _kv_proj(h, w, layer, depth, g_k, gi, kv_prev, batch, seq, tm):
    m, k = h.shape
    tpb = seq // tm
    keep = min(DILATION_GROUPS[gi][0], seq)
    rows = min(keep, tm)
    kv_tiles = keep // rows
    kv_map = lambda i: (layer, i // tpb, jnp.maximum(i % tpb - (tpb - kv_tiles), 0), 0)
    hm_spec = pl.BlockSpec((1, HEADS_PER_GROUP, tm, HEAD_DIM), lambda i: (i // tpb, 0, i % tpb, 0))
    hm_shape = jax.ShapeDtypeStruct((batch, HEADS_PER_GROUP, seq, HEAD_DIM), BF16)
    kv_shape = jax.ShapeDtypeStruct((depth, batch, keep * KV_ROWS, HEAD_DIM), F32)
    win = (_nbytes((tm, k), BF16) + _nbytes((k, TN), BF16) + 2 * _nbytes((tm, TN), BF16)
           + _nbytes((rows * KV_ROWS, HEAD_DIM), F32))
    if kv_prev is None:
        kv_prev = jnp.zeros(kv_shape.shape, F32)
    return pl.pallas_call(
        functools.partial(_kv_kernel, rows=rows),
        grid=(m // tm,),
        in_specs=[pl.BlockSpec((tm, k), lambda i: (i, 0)),
                  pl.BlockSpec((None, k, TN), lambda i: (layer, 0, gi), pipeline_mode=pl.Buffered(1)),
                  pl.BlockSpec((None, k, TN), lambda i: (layer, 0, N_GROUPS + gi),
                               pipeline_mode=pl.Buffered(1)),
                  pl.BlockSpec((1, HEAD_DIM), lambda i: (0, 0)),
                  pl.BlockSpec(memory_space=pl.ANY)],
        out_specs=[hm_spec, hm_spec, pl.BlockSpec((None, None, rows * KV_ROWS, HEAD_DIM), kv_map)],
        out_shape=[hm_shape, hm_shape, kv_shape],
        input_output_aliases={4: 2},
        compiler_params=_params(("arbitrary",), win),
        name="kv_proj",
    )(h, w, w, g_k.reshape(1, HEAD_DIM), kv_prev)


ATTN_BATCH = 4


def _attn_blocks(qkv, bias):
    scores = [_dot_nt(q, k) + bias for q, k, _ in qkv]
    tops = [jnp.max(s, axis=-1, keepdims=True) for s in scores]
    probs = [jnp.exp(s - m).astype(BF16) for s, m in zip(scores, tops)]
    nds = [jnp.dot(e, jnp.concatenate([v, jnp.ones_like(v)], axis=1), preferred_element_type=F32)
           for e, (_, _, v) in zip(probs, qkv)]
    return [(nd[:, :HEAD_DIM] / nd[:, HEAD_DIM:], m + jnp.log(nd[:, HEAD_DIM:])) for nd, m in zip(nds, tops)]


def _attn_kernel(q0_ref, q1_ref, q2_ref, k0_ref, k1_ref, k2_ref, v0_ref, v1_ref, v2_ref, bias_ref, o_ref,
                 q1f, q2f, k1f, k2f, v1f, v2f, o0_sc, o1_sc, o2_sc, l0_sc, l1_sc, l2_sc, *, seq):
    for src, dst in ((q1_ref, q1f), (q2_ref, q2f), (k1_ref, k1f), (k2_ref, k2f), (v1_ref, v1f), (v2_ref, v2f)):
        dst[...] = src[0, 0].astype(F32)

    def run_group(gi, load, o_sc, l_sc, unroll):
        dil = DILATION_GROUPS[gi][1]
        nblk = seq // dil // BLK

        def rows(r, s0, n):
            if dil == 1:
                return pl.ds(s0 if isinstance(s0, int) else pl.multiple_of(s0, BLK), n)
            return pl.ds(s0 * dil + r, n, stride=dil)

        def blocks(todo, first):
            back = 0 if first else BLK
            bias = bias_ref[gi, 0, :, BLK:2 * BLK] if first else bias_ref[gi, 0]
            qkv = [load(rows(r, s0, BLK), rows(r, s0 - back, BLK + back)) for r, s0 in todo]
            for (r, s0), (o, lse) in zip(todo, _attn_blocks(qkv, bias)):
                o_sc[rows(r, s0, BLK), :] = o
                l_sc[rows(r, s0, BLK), :] = lse

        for r0 in range(0, dil, ATTN_BATCH):
            blocks([(r, 0) for r in range(r0, min(dil, r0 + ATTN_BATCH))], True)

        def body(it, carry):
            blocks([(r, (1 + it * unroll + u) * BLK) for u in range(unroll) for r in range(dil)], False)
            return carry

        if nblk > 1:
            lax.fori_loop(0, (nblk - 1) // unroll, body, 0)

    def bf16_load(q_ref, k_ref, v_ref):
        return lambda qr, kr: (q_ref[0, 0, qr, :], k_ref[0, 0, kr, :], v_ref[0, 0, kr, :])

    def f32_load(qf, kf, vf):
        return lambda qr, kr: (qf[qr, :].astype(BF16), kf[kr, :].astype(BF16), vf[kr, :].astype(BF16))

    run_group(0, bf16_load(q0_ref, k0_ref, v0_ref), o0_sc, l0_sc, 5)
    run_group(1, f32_load(q1f, k1f, v1f), o1_sc, l1_sc, 1)
    run_group(2, f32_load(q2f, k2f, v2f), o2_sc, l2_sc, 1)

    def combine(c, carry):
        rows = pl.ds(pl.multiple_of(c * BLK, BLK), BLK)
        l0, l1, l2 = l0_sc[rows, :], l1_sc[rows, :], l2_sc[rows, :]
        top = jnp.maximum(jnp.maximum(l0, l1), l2)
        w0, w1, w2 = jnp.exp(l0 - top), jnp.exp(l1 - top), jnp.exp(l2 - top)
        mixed = (w0 * o0_sc[rows, :] + w1 * o1_sc[rows, :] + w2 * o2_sc[rows, :]) / (w0 + w1 + w2)
        o_ref[0, rows, :] = mixed.astype(o_ref.dtype)
        return carry

    lax.fori_loop(0, seq // BLK, combine, 0)


def _attention(q_hm, ks, vs, bias_tab, batch, seq):
    slab = (1, 1, seq, HEAD_DIM)
    specs = [pl.BlockSpec(slab, lambda b, h, gi=gi: (b, gi * HEADS_PER_GROUP + h, 0, 0))
             for gi in range(N_GROUPS)]
    specs += [pl.BlockSpec(slab, lambda b, h: (b, h, 0, 0))] * (2 * N_GROUPS)
    specs.append(pl.BlockSpec((N_GROUPS, 1, BLK, 2 * BLK), lambda b, h: (0, h, 0, 0)))
    win = (10 * _nbytes((seq, HEAD_DIM), BF16) + _nbytes((N_GROUPS, BLK, 2 * BLK), F32)
           + 6 * _nbytes((seq, HEAD_DIM), F32))
    return pl.pallas_call(
        functools.partial(_attn_kernel, seq=seq),
        grid=(batch, HEADS_PER_GROUP),
        in_specs=specs,
        out_specs=pl.BlockSpec((1, seq, HEAD_DIM), lambda b, h: (b, 0, h)),
        out_shape=jax.ShapeDtypeStruct((batch, seq, ATTN_OUT), BF16),
        scratch_shapes=[pltpu.VMEM((seq, HEAD_DIM), F32)] * 12,
        compiler_params=_params(("parallel", "arbitrary"), win),
        name="dilated_attention",
    )(q_hm, q_hm, q_hm, *ks, *vs, bias_tab)


def _sgu_kernel(u_ref, gv_ref, g_ref, w_ref, bt_ref, o_ref, vs_sc):
    gv = gv_ref[...].astype(F32)
    vs_sc[...] = (_rms(gv) * g_ref[...]).astype(BF16)
    tm = gv.shape[0]
    row = lax.broadcasted_iota(jnp.int32, (BLK, BLK), 0)
    col = lax.broadcasted_iota(jnp.int32, (BLK, BLK), 1)
    tril = (row >= col).astype(F32)
    for g in range(GM_GROUPS):
        cols = slice(g * BLK, (g + 1) * BLK)
        wg = (w_ref[g] * tril).astype(BF16)
        bg = bt_ref[:, g:g + 1]
        for c in range(tm // BLK):
            rows = slice(c * BLK, (c + 1) * BLK)
            mix = _dot(wg, vs_sc[rows, cols]) + bg
            o_ref[rows, cols] = (u_ref[rows, cols].astype(F32) * mix).astype(o_ref.dtype)


def _sgu(uv, g_sgu, w_sgu, b_sgu, tm):
    m, c = uv.shape[0], GM_WIDTH
    win = 3 * _nbytes((tm, c), BF16) + _nbytes((GM_GROUPS, BLK, BLK), F32)
    return pl.pallas_call(
        _sgu_kernel,
        grid=(m // tm,),
        in_specs=[pl.BlockSpec((tm, c), lambda i: (i, 0)),
                  pl.BlockSpec((tm, c), lambda i: (i, 1)),
                  pl.BlockSpec((1, c), lambda i: (0, 0)),
                  pl.BlockSpec((GM_GROUPS, BLK, BLK), lambda i: (0, 0, 0)),
                  pl.BlockSpec((BLK, GM_GROUPS), lambda i: (0, 0))],
        out_specs=pl.BlockSpec((tm, c), lambda i: (i, 0)),
        out_shape=jax.ShapeDtypeStruct((m, c), BF16),
        scratch_shapes=[pltpu.VMEM((tm, c), BF16)],
        compiler_params=_params(("parallel",), win),
        name="spatial_gating",
    )(uv, uv, g_sgu.reshape(1, c), w_sgu, b_sgu.T)


def _merge_kernel(a_ref, s_ref, wa_ref, wb_ref, ga_ref, gb_ref, o_ref):
    a = _dot(a_ref[...], wa_ref[...])
    b = _dot(s_ref[...], wb_ref[...])
    o_ref[...] = (ga_ref[...].astype(F32) * a + gb_ref[...].astype(F32) * b).astype(o_ref.dtype)


def _merge(attn, sgu, wa, wb, layer, ga, gb, gb_col0, out_dtype, tm):
    m = attn.shape[0]
    win = (_nbytes((tm, ATTN_OUT), attn.dtype) + _nbytes((tm, GM_WIDTH), sgu.dtype)
           + _nbytes((ATTN_OUT + GM_WIDTH, TN), BF16) + 2 * _nbytes((tm, TN), ga.dtype)
           + _nbytes((tm, TN), out_dtype))
    row = lambda width: pl.BlockSpec((tm, width), lambda i, j: (i, 0))
    tile = pl.BlockSpec((tm, TN), lambda i, j: (i, j))
    return pl.pallas_call(
        _merge_kernel,
        grid=(m // tm, D_MODEL // TN),
        in_specs=[row(ATTN_OUT), row(GM_WIDTH),
                  _wspec(layer, ATTN_OUT, TN, lambda i, j: j),
                  _wspec(layer, GM_WIDTH, TN, lambda i, j: j), tile,
                  pl.BlockSpec((tm, TN), lambda i, j: (i, j + gb_col0))],
        out_specs=tile,
        out_shape=jax.ShapeDtypeStruct((m, D_MODEL), out_dtype),
        compiler_params=_params(("parallel", "arbitrary"), win),
        name="branch_merge",
    )(attn, sgu, wa, wb, ga, gb)


def _mm_res_kernel(x_ref, w_ref, r_ref, o_ref, *, k_rows):
    x, w = x_ref[...], w_ref[...]
    tk = w.shape[0]
    if k_rows % tk:
        k0 = pl.program_id(2) * tk
        w = jnp.where(k0 + lax.broadcasted_iota(jnp.int32, w.shape, 0) < k_rows, w, jnp.zeros_like(w))
        x = jnp.where(k0 + lax.broadcasted_iota(jnp.int32, x.shape, 1) < k_rows, x, jnp.zeros_like(x))
    part = _dot(x, w)

    @pl.when(pl.program_id(2) == 0)
    def _():
        o_ref[...] = r_ref[...] + part

    @pl.when(pl.program_id(2) > 0)
    def _():
        o_ref[...] += part


def _mm_res(x, w, layer, res, tm, tk):
    m, k = x.shape
    n = w.shape[2]
    win = _nbytes((tm, tk), x.dtype) + _nbytes((tk, TN), BF16) + 2 * _nbytes((tm, TN), F32)
    return pl.pallas_call(
        functools.partial(_mm_res_kernel, k_rows=k),
        grid=(m // tm, n // TN, pl.cdiv(k, tk)),
        in_specs=[pl.BlockSpec((tm, tk), lambda i, j, kk: (i, kk)),
                  pl.BlockSpec((None, tk, TN), lambda i, j, kk: (layer, kk, j)),
                  pl.BlockSpec((tm, TN), lambda i, j, kk: (i, j))],
        out_specs=pl.BlockSpec((tm, TN), lambda i, j, kk: (i, j)),
        out_shape=jax.ShapeDtypeStruct((m, n), F32),
        compiler_params=_params(("parallel", "parallel", "arbitrary"), win),
        name="matmul_residual",
    )(x, w, res)


FFN_TN = 512
GLU_TN = 256


def _glu_kernel(x_ref, xs_ref, wg_ref, wu_ref, o_ref, os_ref, wbg, wbu):
    @pl.when(pl.program_id(1) == 0)
    def _():
        _resident_cast([wg_ref, wu_ref], [wbg, wbu])
        xs = xs_ref[...]
        os_ref[...] = _silu(_dot(xs, wbg[...])) * _dot(xs, wbu[...])

    x = x_ref[...]
    o_ref[...] = (_silu(_dot(x, wbg[...])) * _dot(x, wbu[...])).astype(o_ref.dtype)


def _glu(h, hs, wg, wu, layer, tm):
    (m, k), nb = h.shape, hs.shape[0]
    n = wg.shape[2]
    win = _nbytes((tm, k), BF16) + 2 * _nbytes((k, GLU_TN), F32) + _nbytes((tm, GLU_TN), BF16)
    wspec = _wspec(layer, k, GLU_TN, lambda j, i: j)
    return pl.pallas_call(
        _glu_kernel,
        grid=(n // GLU_TN, m // tm),
        in_specs=[pl.BlockSpec((tm, k), lambda j, i: (i, 0)),
                  pl.BlockSpec((nb, k), lambda j, i: (0, 0)), wspec, wspec],
        out_specs=[pl.BlockSpec((tm, GLU_TN), lambda j, i: (i, j)),
                   pl.BlockSpec((nb, GLU_TN), lambda j, i: (0, j))],
        out_shape=[jax.ShapeDtypeStruct((m, n), BF16), jax.ShapeDtypeStruct((nb, n), F32)],
        scratch_shapes=[pltpu.VMEM((k, GLU_TN), BF16)] * 2,
        compiler_params=_params(("arbitrary", "arbitrary"), win, 2 * _nbytes((k, GLU_TN), BF16)),
        name="swiglu_up",
    )(h, hs, wg, wu)


def _ple_kernel(h_ref, p_ref, wg_ref, wp_ref, r_ref, o_ref):
    gate = jax.nn.sigmoid(_dot(h_ref[...], wg_ref[...]))
    o_ref[...] = r_ref[...] + gate * _dot(p_ref[...], wp_ref[...])


def _ple(h, p, wg, wp, layer, res, tm):
    m, k = h.shape
    tn = FFN_TN
    win = (_nbytes((tm, k), h.dtype) + _nbytes((tm, PLE_DIM), F32) + _nbytes((k + PLE_DIM, tn), BF16)
           + 2 * _nbytes((tm, tn), F32))
    tile = pl.BlockSpec((tm, tn), lambda i, j: (i, j))
    return pl.pallas_call(
        _ple_kernel,
        grid=(m // tm, D_MODEL // tn),
        in_specs=[pl.BlockSpec((tm, k), lambda i, j: (i, 0)),
                  pl.BlockSpec((None, tm, PLE_DIM), lambda i, j: (layer, i, 0)),
                  _wspec(layer, k, tn, lambda i, j: j),
                  _wspec(layer, PLE_DIM, tn, lambda i, j: j), tile],
        out_specs=tile,
        out_shape=jax.ShapeDtypeStruct((m, D_MODEL), F32),
        compiler_params=_params(("parallel", "arbitrary"), win),
        name="ple_gate",
    )(h, p, wg, wp, res)


def _sample_mix_kernel(zh_ref, zr_ref, c0_ref, c1_ref, c2_ref, gq_ref, gk_ref, gsgu_ref,
                       w00_ref, b0_ref, bias_ref, bias0_ref,
                       kvn_ref, attn_ref, sguv_ref, sguo_ref, ga_ref, gb_ref, qn_sc, kn_sc):
    nh, hpg = N_ATTN_HEADS, HEADS_PER_GROUP
    qn_sc[...] = (_rms(zh_ref[0, 0:nh, :]) * gq_ref[...]) * Q_SCALE
    kn_sc[...] = _rms(zh_ref[0, nh:2 * nh, :]) * gk_ref[...]

    outs, lses = [], []
    for gi, c_ref in enumerate((c0_ref, c1_ref, c2_ref)):
        hs = slice(gi * hpg, (gi + 1) * hpg)
        q, k_new = qn_sc[hs, :], kn_sc[hs, :]
        v_new = zh_ref[0, 2 * nh + gi * hpg:2 * nh + (gi + 1) * hpg, :]
        kvn_ref[0, gi, 0] = k_new
        kvn_ref[0, gi, 1] = v_new
        kc = c_ref[0, 0, :, 0:hpg, :]
        vc = c_ref[0, 0, :, hpg:2 * hpg, :]
        s = jnp.sum(kc * q[None], axis=-1, keepdims=True) + bias_ref[gi]
        s_new = jnp.sum(q * k_new, axis=-1, keepdims=True) + bias0_ref[gi]
        m = jnp.maximum(jnp.max(s, axis=0), s_new)
        e = jnp.exp(s - m[None])
        e_new = jnp.exp(s_new - m)
        den = jnp.sum(e, axis=0) + e_new
        outs.append((jnp.sum(e * vc, axis=0) + e_new * v_new) / den)
        lses.append(m + jnp.log(den))
    top = jnp.maximum(jnp.maximum(lses[0], lses[1]), lses[2])
    wts = [jnp.exp(l - top) for l in lses]
    attn_ref[0] = (wts[0] * outs[0] + wts[1] * outs[1] + wts[2] * outs[2]) / (wts[0] + wts[1] + wts[2])

    c = GM_WIDTH
    u = _gelu(zr_ref[0, :, 0:c])
    vs = _rms(_gelu(zr_ref[0, :, c:2 * c])) * gsgu_ref[...]
    sguv_ref[0] = vs
    sguo_ref[0] = u * (w00_ref[...] * vs + b0_ref[...])
    ga_ref[0] = jax.nn.sigmoid(zr_ref[0, :, 2 * c:2 * c + D_MODEL])
    gb_ref[0] = jax.nn.sigmoid(zr_ref[0, :, 2 * c + D_MODEL:2 * c + 2 * D_MODEL])


def _sample_mix(z, caches, layer, g_q, g_k, g_sgu, w_sgu, b_sgu, bias_s, bias0_s):
    nb = z.shape[0]
    hpg = HEADS_PER_GROUP
    zh = z[:, :3 * ATTN_WIDTH].reshape(nb, 3 * N_ATTN_HEADS, HEAD_DIM)
    rest = 2 * GM_WIDTH + 2 * D_MODEL
    zr = z[:, 3 * ATTN_WIDTH:].reshape(nb, 1, rest)
    w00 = jnp.repeat(w_sgu[:, 0, 0], BLK).reshape(1, GM_WIDTH)
    b0 = jnp.repeat(b_sgu[:, 0], BLK).reshape(1, GM_WIDTH)
    cache_views, cache_specs = [], []
    for c, (_, dil) in zip(caches, DILATION_GROUPS):
        depth, _, length = c.shape[:3]
        cache_views.append(c.reshape(depth, nb, length // dil, dil * KV_ROWS, HEAD_DIM))
        cache_specs.append(pl.BlockSpec((1, 1, BLK, KV_ROWS, HEAD_DIM), lambda b: (layer, b, 0, 0, 0)))
    vec = lambda width: pl.BlockSpec((1, width), lambda b: (0, 0))
    per_b = lambda width: pl.BlockSpec((1, 1, width), lambda b: (b, 0, 0))
    win = (3 * _nbytes((BLK, KV_ROWS, HEAD_DIM), F32) + _nbytes((1, 4 * rest), F32)
           + _nbytes((N_GROUPS, BLK + 1, hpg, HEAD_DIM), F32))
    return pl.pallas_call(
        _sample_mix_kernel,
        grid=(nb,),
        in_specs=[pl.BlockSpec((1, 3 * N_ATTN_HEADS, HEAD_DIM), lambda b: (b, 0, 0)), per_b(rest),
                  *cache_specs, vec(HEAD_DIM), vec(HEAD_DIM), vec(GM_WIDTH), vec(GM_WIDTH),
                  vec(GM_WIDTH),
                  pl.BlockSpec((N_GROUPS, BLK, hpg, HEAD_DIM), lambda b: (0, 0, 0, 0)),
                  pl.BlockSpec((N_GROUPS, hpg, HEAD_DIM), lambda b: (0, 0, 0))],
        out_specs=[pl.BlockSpec((1, N_GROUPS, 2, hpg, HEAD_DIM), lambda b: (b, 0, 0, 0, 0)),
                   pl.BlockSpec((1, hpg, HEAD_DIM), lambda b: (b, 0, 0)),
                   per_b(GM_WIDTH), per_b(GM_WIDTH), per_b(D_MODEL), per_b(D_MODEL)],
        out_shape=[jax.ShapeDtypeStruct((nb, N_GROUPS, 2, hpg, HEAD_DIM), F32),
                   jax.ShapeDtypeStruct((nb, hpg, HEAD_DIM), F32),
                   jax.ShapeDtypeStruct((nb, 1, GM_WIDTH), F32),
                   jax.ShapeDtypeStruct((nb, 1, GM_WIDTH), F32),
                   jax.ShapeDtypeStruct((nb, 1, D_MODEL), F32),
                   jax.ShapeDtypeStruct((nb, 1, D_MODEL), F32)],
        scratch_shapes=[pltpu.VMEM((N_ATTN_HEADS, HEAD_DIM), F32)] * 2,
        compiler_params=_params(("parallel",), win),
        name="decode_mixers",
    )(zh, zr, *cache_views, g_q.reshape(1, HEAD_DIM), g_k.reshape(1, HEAD_DIM),
      g_sgu.reshape(1, GM_WIDTH), w00, b0, bias_s, bias0_s)


ROLL_TOKENS = 512


def _roll_kernel(chunk_ref, next_ref, new_ref, o_ref):
    kept = chunk_ref.shape[0] - KV_ROWS
    o_ref[0:kept, :] = chunk_ref[KV_ROWS:, :]
    last = pl.program_id(2) == pl.num_programs(2) - 1

    @pl.when(last)
    def _():
        o_ref[kept:, :] = new_ref[...]

    @pl.when(jnp.logical_not(last))
    def _():
        o_ref[kept:, :] = next_ref[...]


def _roll_cache(cache, new_rows):
    depth, nb, length = cache.shape[:3]
    tokens = min(length, ROLL_TOKENS)
    rows = tokens * KV_ROWS
    flat = cache.reshape(depth, nb, length * KV_ROWS, HEAD_DIM)
    token = (None, None, KV_ROWS, HEAD_DIM)
    out = pl.pallas_call(
        _roll_kernel,
        grid=(depth, nb, length // tokens),
        in_specs=[pl.BlockSpec((None, None, rows, HEAD_DIM), lambda d, b, c: (d, b, c, 0)),
                  pl.BlockSpec(token, lambda d, b, c: (d, b, jnp.minimum((c + 1) * tokens, length - 1), 0)),
                  pl.BlockSpec(token, lambda d, b, c: (d, b, 0, 0))],
        out_specs=pl.BlockSpec((None, None, rows, HEAD_DIM), lambda d, b, c: (d, b, c, 0)),
        out_shape=jax.ShapeDtypeStruct(flat.shape, cache.dtype),
        compiler_params=_params(("parallel", "parallel", "arbitrary"), 2 * _nbytes((rows, HEAD_DIM), F32)),
        name="roll_cache",
    )(flat, flat, new_rows.reshape(depth, nb, KV_ROWS, HEAD_DIM))
    return out.reshape(cache.shape)


def _t5_bucket(dist):
    dist = np.asarray(dist)
    max_exact = NUM_BUCKETS // 2
    large = max_exact + (np.log(np.maximum(dist, 1) / max_exact) / np.log(REL_MAX_DIST / max_exact)
                         * (NUM_BUCKETS - max_exact)).astype(np.int32)
    large = np.minimum(large, NUM_BUCKETS - 1)
    return np.where(dist < max_exact, dist, large).astype(np.int32)


def _bias_tables(rel_bias):
    hpg = HEADS_PER_GROUP
    prompt, dec, dec0 = [], [], []
    for gi, (_, dil) in enumerate(DILATION_GROUPS):
        heads = rel_bias[:, gi * hpg:(gi + 1) * hpg].astype(F32)
        by_off = jnp.take(heads, _t5_bucket(np.arange(BLK + 1) * dil), axis=0).T
        w = jnp.concatenate([by_off[:, ::-1], jnp.full((hpg, BLK), NEG_INF, F32)], axis=1)
        tab = jnp.tile(w, (1, BLK))[:, :BLK * 2 * BLK].reshape(hpg, BLK, 2 * BLK)
        prompt.append(tab)
        dec.append(jnp.broadcast_to(by_off[:, :0:-1].T[:, :, None], (BLK, hpg, HEAD_DIM)))
        dec0.append(jnp.broadcast_to(by_off[:, 0:1], (hpg, HEAD_DIM)))
    return jnp.stack(prompt), jnp.stack(dec), jnp.stack(dec0)


def _prepare_weights(w_in, w_attn_br, w_sgu_br, w_out, w_ffn_gate, w_ffn_up, w_ffn_down,
                     w_ple_gate, w_ple_proj):
    return dict(
        w_in=w_in, w_gate=w_ffn_gate, w_up=w_ffn_up,
        w_kv=w_in[:, :, ATTN_WIDTH:3 * ATTN_WIDTH].astype(BF16),
        w_attn_br=w_attn_br.astype(BF16), w_sgu_br=w_sgu_br.astype(BF16), w_out=w_out.astype(BF16),
        w_down=w_ffn_down.astype(BF16),
        w_ple_gate=w_ple_gate.astype(BF16), w_ple_proj=w_ple_proj.astype(BF16))


DOWN_TK = 2816


def _layer(xp, xs, pp, ps, caches, wts, layer, depth, kv_prev, tables, g_mix, g_q, g_k, g_sgu, w_sgu,
           b_sgu, g_ffn, g_ple, batch, seq):
    tm, nb = 1024, xs.shape[0]
    bias_tab, bias_s, bias0_s = tables
    w_in = wts["w_in"]

    h = _rmsnorm(xp, g_mix, BF16, 256)
    hs = _rmsnorm(xs, g_mix, F32, nb)
    q_hm, zq = _q_proj(h, hs, w_in, layer, g_q, batch, seq, tm)
    uv, zu = _proj_act(h, hs, w_in, layer, COL_U * TN // PTN, 2 * GM_WIDTH // PTN, "gelu", tm)
    gates, zg = _proj_act(h, hs, w_in, layer, COL_GATE * TN // PTN, 2 * D_MODEL // PTN, "sigmoid", tm)
    ks, vs, kvs = [], [], []
    for gi in range(N_GROUPS):
        k_hm, v_hm, kv = _kv_proj(h, wts["w_kv"], layer, depth, g_k, gi, kv_prev[gi], batch, seq, 512)
        ks.append(k_hm)
        vs.append(v_hm)
        kvs.append(kv)
    zkv = _proj_rows(hs, wts["w_kv"], layer, 2 * N_GROUPS)

    attn_p = _attention(q_hm, ks, vs, bias_tab, batch, seq).reshape(batch * seq, ATTN_OUT)
    sgu_p = _sgu(uv, g_sgu, w_sgu, b_sgu, 512)
    z = jnp.concatenate([zq, zkv, zu, zg], axis=1)
    kvn, attn_s, sguv, sgu_s, ga_s, gb_s = _sample_mix(z, caches, layer, g_q, g_k, g_sgu, w_sgu, b_sgu,
                                                       bias_s, bias0_s)

    wa, wb = wts["w_attn_br"], wts["w_sgu_br"]
    merged_p = _merge(attn_p, sgu_p, wa, wb, layer, gates, gates, D_MODEL // TN, BF16, tm)
    merged_s = _merge(attn_s.reshape(nb, ATTN_OUT), sgu_s.reshape(nb, GM_WIDTH), wa, wb, layer,
                      ga_s.reshape(nb, D_MODEL), gb_s.reshape(nb, D_MODEL), 0, F32, nb)
    xp = _mm_res(merged_p, wts["w_out"], layer, xp, tm, D_MODEL)
    xs = _mm_res(merged_s, wts["w_out"], layer, xs, nb, D_MODEL)

    ff_p, ff_s = _glu(_rmsnorm(xp, g_ffn, BF16, 256), _rmsnorm(xs, g_ffn, F32, nb),
                      wts["w_gate"], wts["w_up"], layer, tm)
    xp = _mm_res(ff_p, wts["w_down"], layer, xp, tm, DOWN_TK)
    xs = _mm_res(ff_s, wts["w_down"], layer, xs, nb, DOWN_TK)

    xp = _ple(_rmsnorm(xp, g_ple, BF16, 256), pp, wts["w_ple_gate"], wts["w_ple_proj"], layer, xp, tm)
    xs = _ple(_rmsnorm(xs, g_ple, F32, nb), ps, wts["w_ple_gate"], wts["w_ple_proj"], layer, xs, nb)
    return xp, xs, kvs, kvn, sguv


def kernel(x_prompt, x_sample, cache_kv_w128, cache_kv_w512, cache_kv_w2048, p_prompt, p_sample,
           rel_bias, g_mix, w_in, g_q, g_k, g_sgu, w_sgu, b_sgu, w_attn_br, w_sgu_br, w_out, g_ffn,
           w_ffn_gate, w_ffn_up, w_ffn_down, g_ple, w_ple_gate, w_ple_proj):
    batch, seq, _ = x_prompt.shape
    nb = x_sample.shape[0]
    depth = w_in.shape[0]
    caches = (cache_kv_w128, cache_kv_w512, cache_kv_w2048)
    tables = _bias_tables(rel_bias)
    wts = _prepare_weights(w_in, w_attn_br, w_sgu_br, w_out, w_ffn_gate, w_ffn_up, w_ffn_down,
                           w_ple_gate, w_ple_proj)

    xp = x_prompt.reshape(batch * seq, D_MODEL)
    xs = x_sample.reshape(nb, D_MODEL)
    pp = p_prompt.reshape(depth, batch * seq, PLE_DIM)
    ps = p_sample.reshape(depth, nb, PLE_DIM)
    kv_p = [None] * N_GROUPS
    kv_new = [[] for _ in range(N_GROUPS)]
    sgu_v = []
    for i in range(depth):
        norms = (g_mix[i], g_q[i], g_k[i], g_sgu[i], w_sgu[i], b_sgu[i], g_ffn[i], g_ple[i])
        xp, xs, kv_p, kvn, sguv = _layer(xp, xs, pp, ps, caches, wts, i, depth, kv_p, tables, *norms,
                                         batch, seq)
        for gi in range(N_GROUPS):
            kv_new[gi].append(kvn[:, gi])
        sgu_v.append(sguv)

    kv_p = [kv.reshape(depth, batch, -1, 2, HEADS_PER_GROUP, HEAD_DIM) for kv in kv_p]
    kv_s = [_roll_cache(c, jnp.stack(kv_new[gi], axis=0)) for gi, c in enumerate(caches)]
    return (xp.reshape(batch, seq, D_MODEL), xs.reshape(nb, 1, D_MODEL), *kv_p, *kv_s,
            jnp.stack(sgu_v, axis=0))
```

```python
import functools

import numpy as np
import jax
import jax.numpy as jnp
from jax import lax
from jax.experimental import pallas as pl
from jax.experimental.pallas import tpu as pltpu

D_MODEL = 4096
HEAD_DIM = 128
DILATION_GROUPS = ((128, 1), (512, 4), (2048, 16))
N_GROUPS = 3
HEADS_PER_GROUP = 8
N_ATTN_HEADS = 24
ATTN_WIDTH = 3072
ATTN_OUT = 1024
BLK = 128
KV_ROWS = 2 * HEADS_PER_GROUP
GM_WIDTH = 2048
GM_GROUPS = 16
PLE_DIM = 256
NUM_BUCKETS = 32
REL_MAX_DIST = 2048
NORM_EPS = 1e-6
NEG_INF = -1e30
Q_SCALE = HEAD_DIM ** -0.5

TN = 1024
COL_U, COL_GATE = 9, 13

V7X_VMEM_REQUEST_CAP = 60000 * 1024
COMPILER_TEMP_BYTES = 16 * 1024 * 1024

F32 = jnp.float32
BF16 = jnp.bfloat16


def _params(sems, window_bytes, scratch_bytes=0):
    limit = min(V7X_VMEM_REQUEST_CAP, 2 * window_bytes + scratch_bytes + COMPILER_TEMP_BYTES)
    return pltpu.CompilerParams(dimension_semantics=sems, vmem_limit_bytes=int(limit))


def _nbytes(shape, dtype):
    return int(np.prod(shape)) * jnp.dtype(dtype).itemsize


def _gelu(x):
    return x * (0.5 * (1.0 + jnp.tanh(0.7978845608028654 * (x + 0.044715 * (x * x * x)))))


def _silu(x):
    return x * jax.nn.sigmoid(x)


_ACT = {"gelu": _gelu, "sigmoid": jax.nn.sigmoid, "none": lambda x: x}


def _rms(x):
    return x * lax.rsqrt(jnp.mean(x * x, axis=-1, keepdims=True) + NORM_EPS)


def _dot(a, b):
    return jnp.dot(a.astype(BF16), b.astype(BF16), preferred_element_type=F32)


def _dot_nt(a, b):
    return lax.dot_general(a, b, (((1,), (1,)), ((), ())), preferred_element_type=F32)


def _wspec(layer, k, tn, col_map):
    return pl.BlockSpec((None, k, tn), lambda *ids: (layer, 0, col_map(*ids)))


def _rmsnorm_kernel(x_ref, g_ref, o_ref):
    o_ref[...] = (_rms(x_ref[...]) * g_ref[...]).astype(o_ref.dtype)


def _rmsnorm(x, g, out_dtype, tm):
    m, d = x.shape
    return pl.pallas_call(
        _rmsnorm_kernel,
        grid=(m // tm,),
        in_specs=[pl.BlockSpec((tm, d), lambda i: (i, 0)), pl.BlockSpec((1, d), lambda i: (0, 0))],
        out_specs=pl.BlockSpec((tm, d), lambda i: (i, 0)),
        out_shape=jax.ShapeDtypeStruct((m, d), out_dtype),
        compiler_params=_params(("parallel",), _nbytes((tm, d), F32) + _nbytes((tm, d), out_dtype)),
        name="rmsnorm",
    )(x, g.reshape(1, d))


def _proj_rows_kernel(x_ref, w_ref, o_ref):
    o_ref[...] = _dot(x_ref[...], w_ref[...])


def _proj_rows(h, w, layer, n_tiles):
    m, k = h.shape
    win = _nbytes((m, k), h.dtype) + _nbytes((k, TN), BF16) + _nbytes((m, TN), F32)
    return pl.pallas_call(
        _proj_rows_kernel,
        grid=(n_tiles,),
        in_specs=[pl.BlockSpec((m, k), lambda j: (0, 0)), _wspec(layer, k, TN, lambda j: j)],
        out_specs=pl.BlockSpec((m, TN), lambda j: (0, j)),
        out_shape=jax.ShapeDtypeStruct((m, n_tiles * TN), F32),
        compiler_params=_params(("parallel",), win),
        name="proj_rows",
    )(h, w)


PTN = 1024
RES_TM = 512


def _resident_cast(w_refs, wb_refs):
    for w_ref, wb_ref in zip(w_refs, wb_refs):
        wb_ref[...] = w_ref[...].astype(BF16)


def _proj_act_kernel(x_ref, xs_ref, w_ref, o_ref, os_ref, wb, *, act):
    @pl.when(pl.program_id(1) == 0)
    def _():
        _resident_cast([w_ref], [wb])
        os_ref[...] = _dot(xs_ref[...], wb[...])

    o_ref[...] = _ACT[act](_dot(x_ref[...], wb[...])).astype(o_ref.dtype)


def _proj_act(h, hs, w, layer, col0, n_tiles, act, tm):
    (m, k), nb = h.shape, hs.shape[0]
    win = _nbytes((tm, k), BF16) + _nbytes((k, PTN), F32) + _nbytes((tm, PTN), BF16)
    return pl.pallas_call(
        functools.partial(_proj_act_kernel, act=act),
        grid=(n_tiles, m // tm),
        in_specs=[pl.BlockSpec((tm, k), lambda j, i: (i, 0)),
                  pl.BlockSpec((nb, k), lambda j, i: (0, 0)),
                  _wspec(layer, k, PTN, lambda j, i: j + col0)],
        out_specs=[pl.BlockSpec((tm, PTN), lambda j, i: (i, j)),
                   pl.BlockSpec((nb, PTN), lambda j, i: (0, j))],
        out_shape=[jax.ShapeDtypeStruct((m, n_tiles * PTN), BF16),
                   jax.ShapeDtypeStruct((nb, n_tiles * PTN), F32)],
        scratch_shapes=[pltpu.VMEM((k, PTN), BF16)],
        compiler_params=_params(("arbitrary", "arbitrary"), win, _nbytes((k, PTN), BF16)),
        name="proj_" + act,
    )(h, hs, w)


def _q_kernel(x_ref, xs_ref, w_ref, g_ref, o_ref, os_ref, wb):
    @pl.when(pl.program_id(1) == 0)
    def _():
        _resident_cast([w_ref], [wb])
        os_ref[...] = _dot(xs_ref[...], wb[...])

    acc = _dot(x_ref[...], wb[...])
    for hh in range(PTN // HEAD_DIM):
        a = acc[:, hh * HEAD_DIM:(hh + 1) * HEAD_DIM]
        o_ref[0, hh] = ((_rms(a) * g_ref[...]) * Q_SCALE).astype(o_ref.dtype)


def _q_proj(h, hs, w, layer, g_q, batch, seq, tm):
    (m, k), nb = h.shape, hs.shape[0]
    tpb = seq // tm
    heads = PTN // HEAD_DIM
    win = _nbytes((tm, k), BF16) + _nbytes((k, PTN), F32) + _nbytes((tm, PTN), BF16)
    return pl.pallas_call(
        _q_kernel,
        grid=(ATTN_WIDTH // PTN, m // tm),
        in_specs=[pl.BlockSpec((tm, k), lambda j, i: (i, 0)),
                  pl.BlockSpec((nb, k), lambda j, i: (0, 0)),
                  _wspec(layer, k, PTN, lambda j, i: j),
                  pl.BlockSpec((1, HEAD_DIM), lambda j, i: (0, 0))],
        out_specs=[pl.BlockSpec((1, heads, tm, HEAD_DIM), lambda j, i: (i // tpb, j, i % tpb, 0)),
                   pl.BlockSpec((nb, PTN), lambda j, i: (0, j))],
        out_shape=[jax.ShapeDtypeStruct((batch, N_ATTN_HEADS, seq, HEAD_DIM), BF16),
                   jax.ShapeDtypeStruct((nb, ATTN_WIDTH), F32)],
        scratch_shapes=[pltpu.VMEM((k, PTN), BF16)],
        compiler_params=_params(("arbitrary", "arbitrary"), win, _nbytes((k, PTN), BF16)),
        name="q_proj",
    )(h, hs, w, g_q.reshape(1, HEAD_DIM))


def _kv_kernel(x_ref, wk_ref, wv_ref, g_ref, kv_prev_ref, khm_ref, vhm_ref, kv_ref, *, rows):
    del kv_prev_ref
    x = x_ref[...]
    k = _dot(x, wk_ref[...])
    v = _dot(x, wv_ref[...])
    tm = x.shape[0]
    for hh in range(HEADS_PER_GROUP):
        sl = slice(hh * HEAD_DIM, (hh + 1) * HEAD_DIM)
        kn = _rms(k[:, sl]) * g_ref[...]
        vh = v[:, sl]
        khm_ref[0, hh] = kn.astype(BF16)
        vhm_ref[0, hh] = vh.astype(BF16)
        kv_ref[pl.ds(hh, rows, stride=KV_ROWS), :] = kn[tm - rows:]
        kv_ref[pl.ds(HEADS_PER_GROUP + hh, rows, stride=KV_ROWS), :] = vh[tm - rows:]


def _kv_proj(h, w, layer, depth, g_k, gi, kv_prev, batch, seq, tm):
    m, k = h.shape
    tpb = seq // tm
    keep = min(DILATION_GROUPS[gi][0], seq)
    rows = min(keep, tm)
    kv_tiles = keep // rows
    kv_map = lambda i: (layer, i // tpb, jnp.maximum(i % tpb - (tpb - kv_tiles), 0), 0)
    hm_spec = pl.BlockSpec((1, HEADS_PER_GROUP, tm, HEAD_DIM), lambda i: (i // tpb, 0, i % tpb, 0))
    hm_shape = jax.ShapeDtypeStruct((batch, HEADS_PER_GROUP, seq, HEAD_DIM), BF16)
    kv_shape = jax.ShapeDtypeStruct((depth, batch, keep * KV_ROWS, HEAD_DIM), F32)
    win = (_nbytes((tm, k), BF16) + _nbytes((k, TN), BF16) + 2 * _nbytes((tm, TN), BF16)
           + _nbytes((rows * KV_ROWS, HEAD_DIM), F32))
    if kv_prev is None:
        kv_prev = jnp.zeros(kv_shape.shape, F32)
    return pl.pallas_call(
        functools.partial(_kv_kernel, rows=rows),
        grid=(m // tm,),
        in_specs=[pl.BlockSpec((tm, k), lambda i: (i, 0)),
                  pl.BlockSpec((None, k, TN), lambda i: (layer, 0, gi), pipeline_mode=pl.Buffered(1)),
                  pl.BlockSpec((None, k, TN), lambda i: (layer, 0, N_GROUPS + gi),
                               pipeline_mode=pl.Buffered(1)),
                  pl.BlockSpec((1, HEAD_DIM), lambda i: (0, 0)),
                  pl.BlockSpec(memory_space=pl.ANY)],
        out_specs=[hm_spec, hm_spec, pl.BlockSpec((None, None, rows * KV_ROWS, HEAD_DIM), kv_map)],
        out_shape=[hm_shape, hm_shape, kv_shape],
        input_output_aliases={4: 2},
        compiler_params=_params(("arbitrary",), win),
        name="kv_proj",
    )(h, w, w, g_k.reshape(1, HEAD_DIM), kv_prev)


ATTN_BATCH = 4


def _attn_blocks(qkv, bias):
    scores = [_dot_nt(q, k) + bias for q, k, _ in qkv]
    tops = [jnp.max(s, axis=-1, keepdims=True) for s in scores]
    probs = [jnp.exp(s - m).astype(BF16) for s, m in zip(scores, tops)]
    nds = [jnp.dot(e, jnp.concatenate([v, jnp.ones_like(v)], axis=1), preferred_element_type=F32)
           for e, (_, _, v) in zip(probs, qkv)]
    return [(nd[:, :HEAD_DIM] / nd[:, HEAD_DIM:], m + jnp.log(nd[:, HEAD_DIM:])) for nd, m in zip(nds, tops)]


def _attn_kernel(q0_ref, q1_ref, q2_ref, k0_ref, k1_ref, k2_ref, v0_ref, v1_ref, v2_ref, bias_ref, o_ref,
                 q1f, q2f, k1f, k2f, v1f, v2f, o0_sc, o1_sc, o2_sc, l0_sc, l1_sc, l2_sc, *, seq):
    for src, dst in ((q1_ref, q1f), (q2_ref, q2f), (k1_ref, k1f), (k2_ref, k2f), (v1_ref, v1f), (v2_ref, v2f)):
        dst[...] = src[0, 0].astype(F32)

    def run_group(gi, load, o_sc, l_sc, unroll):
        dil = DILATION_GROUPS[gi][1]
        nblk = seq // dil // BLK

        def rows(r, s0, n):
            if dil == 1:
                return pl.ds(s0 if isinstance(s0, int) else pl.multiple_of(s0, BLK), n)
            return pl.ds(s0 * dil + r, n, stride=dil)

        def blocks(todo, first):
            back = 0 if first else BLK
            bias = bias_ref[gi, 0, :, BLK:2 * BLK] if first else bias_ref[gi, 0]
            qkv = [load(rows(r, s0, BLK), rows(r, s0 - back, BLK + back)) for r, s0 in todo]
            for (r, s0), (o, lse) in zip(todo, _attn_blocks(qkv, bias)):
                o_sc[rows(r, s0, BLK), :] = o
                l_sc[rows(r, s0, BLK), :] = lse

        for r0 in range(0, dil, ATTN_BATCH):
            blocks([(r, 0) for r in range(r0, min(dil, r0 + ATTN_BATCH))], True)

        def body(it, carry):
            blocks([(r, (1 + it * unroll + u) * BLK) for u in range(unroll) for r in range(dil)], False)
            return carry

        if nblk > 1:
            lax.fori_loop(0, (nblk - 1) // unroll, body, 0)

    def bf16_load(q_ref, k_ref, v_ref):
        return lambda qr, kr: (q_ref[0, 0, qr, :], k_ref[0, 0, kr, :], v_ref[0, 0, kr, :])

    def f32_load(qf, kf, vf):
        return lambda qr, kr: (qf[qr, :].astype(BF16), kf[kr, :].astype(BF16), vf[kr, :].astype(BF16))

    run_group(0, bf16_load(q0_ref, k0_ref, v0_ref), o0_sc, l0_sc, 5)
    run_group(1, f32_load(q1f, k1f, v1f), o1_sc, l1_sc, 1)
    run_group(2, f32_load(q2f, k2f, v2f), o2_sc, l2_sc, 1)

    def combine(c, carry):
        rows = pl.ds(pl.multiple_of(c * BLK, BLK), BLK)
        l0, l1, l2 = l0_sc[rows, :], l1_sc[rows, :], l2_sc[rows, :]
        top = jnp.maximum(jnp.maximum(l0, l1), l2)
        w0, w1, w2 = jnp.exp(l0 - top), jnp.exp(l1 - top), jnp.exp(l2 - top)
        mixed = (w0 * o0_sc[rows, :] + w1 * o1_sc[rows, :] + w2 * o2_sc[rows, :]) / (w0 + w1 + w2)
        o_ref[0, rows, :] = mixed.astype(o_ref.dtype)
        return carry

    lax.fori_loop(0, seq // BLK, combine, 0)


def _attention(q_hm, ks, vs, bias_tab, batch, seq):
    slab = (1, 1, seq, HEAD_DIM)
    specs = [pl.BlockSpec(slab, lambda b, h, gi=gi: (b, gi * HEADS_PER_GROUP + h, 0, 0))
             for gi in range(N_GROUPS)]
    specs += [pl.BlockSpec(slab, lambda b, h: (b, h, 0, 0))] * (2 * N_GROUPS)
    specs.append(pl.BlockSpec((N_GROUPS, 1, BLK, 2 * BLK), lambda b, h: (0, h, 0, 0)))
    win = (10 * _nbytes((seq, HEAD_DIM), BF16) + _nbytes((N_GROUPS, BLK, 2 * BLK), F32)
           + 6 * _nbytes((seq, HEAD_DIM), F32))
    return pl.pallas_call(
        functools.partial(_attn_kernel, seq=seq),
        grid=(batch, HEADS_PER_GROUP),
        in_specs=specs,
        out_specs=pl.BlockSpec((1, seq, HEAD_DIM), lambda b, h: (b, 0, h)),
        out_shape=jax.ShapeDtypeStruct((batch, seq, ATTN_OUT), BF16),
        scratch_shapes=[pltpu.VMEM((seq, HEAD_DIM), F32)] * 12,
        compiler_params=_params(("parallel", "arbitrary"), win),
        name="dilated_attention",
    )(q_hm, q_hm, q_hm, *ks, *vs, bias_tab)


def _sgu_kernel(u_ref, gv_ref, g_ref, w_ref, bt_ref, o_ref, vs_sc):
    gv = gv_ref[...].astype(F32)
    vs_sc[...] = (_rms(gv) * g_ref[...]).astype(BF16)
    tm = gv.shape[0]
    row = lax.broadcasted_iota(jnp.int32, (BLK, BLK), 0)
    col = lax.broadcasted_iota(jnp.int32, (BLK, BLK), 1)
    tril = (row >= col).astype(F32)
    for g in range(GM_GROUPS):
        cols = slice(g * BLK, (g + 1) * BLK)
        wg = (w_ref[g] * tril).astype(BF16)
        bg = bt_ref[:, g:g + 1]
        for c in range(tm // BLK):
            rows = slice(c * BLK, (c + 1) * BLK)
            mix = _dot(wg, vs_sc[rows, cols]) + bg
            o_ref[rows, cols] = (u_ref[rows, cols].astype(F32) * mix).astype(o_ref.dtype)


def _sgu(uv, g_sgu, w_sgu, b_sgu, tm):
    m, c = uv.shape[0], GM_WIDTH
    win = 3 * _nbytes((tm, c), BF16) + _nbytes((GM_GROUPS, BLK, BLK), F32)
    return pl.pallas_call(
        _sgu_kernel,
        grid=(m // tm,),
        in_specs=[pl.BlockSpec((tm, c), lambda i: (i, 0)),
                  pl.BlockSpec((tm, c), lambda i: (i, 1)),
                  pl.BlockSpec((1, c), lambda i: (0, 0)),
                  pl.BlockSpec((GM_GROUPS, BLK, BLK), lambda i: (0, 0, 0)),
                  pl.BlockSpec((BLK, GM_GROUPS), lambda i: (0, 0))],
        out_specs=pl.BlockSpec((tm, c), lambda i: (i, 0)),
        out_shape=jax.ShapeDtypeStruct((m, c), BF16),
        scratch_shapes=[pltpu.VMEM((tm, c), BF16)],
        compiler_params=_params(("parallel",), win),
        name="spatial_gating",
    )(uv, uv, g_sgu.reshape(1, c), w_sgu, b_sgu.T)


def _merge_kernel(a_ref, s_ref, wa_ref, wb_ref, ga_ref, gb_ref, o_ref):
    a = _dot(a_ref[...], wa_ref[...])
    b = _dot(s_ref[...], wb_ref[...])
    o_ref[...] = (ga_ref[...].astype(F32) * a + gb_ref[...].astype(F32) * b).astype(o_ref.dtype)


def _merge(attn, sgu, wa, wb, layer, ga, gb, gb_col0, out_dtype, tm):
    m = attn.shape[0]
    win = (_nbytes((tm, ATTN_OUT), attn.dtype) + _nbytes((tm, GM_WIDTH), sgu.dtype)
           + _nbytes((ATTN_OUT + GM_WIDTH, TN), BF16) + 2 * _nbytes((tm, TN), ga.dtype)
           + _nbytes((tm, TN), out_dtype))
    row = lambda width: pl.BlockSpec((tm, width), lambda i, j: (i, 0))
    tile = pl.BlockSpec((tm, TN), lambda i, j: (i, j))
    return pl.pallas_call(
        _merge_kernel,
        grid=(m // tm, D_MODEL // TN),
        in_specs=[row(ATTN_OUT), row(GM_WIDTH),
                  _wspec(layer, ATTN_OUT, TN, lambda i, j: j),
                  _wspec(layer, GM_WIDTH, TN, lambda i, j: j), tile,
                  pl.BlockSpec((tm, TN), lambda i, j: (i, j + gb_col0))],
        out_specs=tile,
        out_shape=jax.ShapeDtypeStruct((m, D_MODEL), out_dtype),
        compiler_params=_params(("parallel", "arbitrary"), win),
        name="branch_merge",
    )(attn, sgu, wa, wb, ga, gb)


def _mm_res_kernel(x_ref, w_ref, r_ref, o_ref, *, k_rows):
    x, w = x_ref[...], w_ref[...]
    tk = w.shape[0]
    if k_rows % tk:
        k0 = pl.program_id(2) * tk
        w = jnp.where(k0 + lax.broadcasted_iota(jnp.int32, w.shape, 0) < k_rows, w, jnp.zeros_like(w))
        x = jnp.where(k0 + lax.broadcasted_iota(jnp.int32, x.shape, 1) < k_rows, x, jnp.zeros_like(x))
    part = _dot(x, w)

    @pl.when(pl.program_id(2) == 0)
    def _():
        o_ref[...] = r_ref[...] + part

    @pl.when(pl.program_id(2) > 0)
    def _():
        o_ref[...] += part


def _mm_res(x, w, layer, res, tm, tk):
    m, k = x.shape
    n = w.shape[2]
    win = _nbytes((tm, tk), x.dtype) + _nbytes((tk, TN), BF16) + 2 * _nbytes((tm, TN), F32)
    return pl.pallas_call(
        functools.partial(_mm_res_kernel, k_rows=k),
        grid=(m // tm, n // TN, pl.cdiv(k, tk)),
        in_specs=[pl.BlockSpec((tm, tk), lambda i, j, kk: (i, kk)),
                  pl.BlockSpec((None, tk, TN), lambda i, j, kk: (layer, kk, j)),
                  pl.BlockSpec((tm, TN), lambda i, j, kk: (i, j))],
        out_specs=pl.BlockSpec((tm, TN), lambda i, j, kk: (i, j)),
        out_shape=jax.ShapeDtypeStruct((m, n), F32),
        compiler_params=_params(("parallel", "parallel", "arbitrary"), win),
        name="matmul_residual",
    )(x, w, res)


GLU_TN = 512


def _glu_kernel(x_ref, xs_ref, wg_ref, wu_ref, o_ref, os_ref, wbg, wbu):
    @pl.when(pl.program_id(1) == 0)
    def _():
        _resident_cast([wg_ref, wu_ref], [wbg, wbu])
        xs = xs_ref[...]
        os_ref[...] = _silu(_dot(xs, wbg[...])) * _dot(xs, wbu[...])

    x = x_ref[...]
    o_ref[...] = (_silu(_dot(x, wbg[...])) * _dot(x, wbu[...])).astype(o_ref.dtype)


def _glu(h, hs, wg, wu, layer, tm):
    (m, k), nb = h.shape, hs.shape[0]
    n = wg.shape[2]
    win = _nbytes((tm, k), BF16) + 2 * _nbytes((k, GLU_TN), F32) + _nbytes((tm, GLU_TN), BF16)
    wspec = _wspec(layer, k, GLU_TN, lambda j, i: j)
    return pl.pallas_call(
        _glu_kernel,
        grid=(pl.cdiv(n, GLU_TN), m // tm),
        in_specs=[pl.BlockSpec((tm, k), lambda j, i: (i, 0)),
                  pl.BlockSpec((nb, k), lambda j, i: (0, 0)), wspec, wspec],
        out_specs=[pl.BlockSpec((tm, GLU_TN), lambda j, i: (i, j)),
                   pl.BlockSpec((nb, GLU_TN), lambda j, i: (0, j))],
        out_shape=[jax.ShapeDtypeStruct((m, n), BF16), jax.ShapeDtypeStruct((nb, n), F32)],
        scratch_shapes=[pltpu.VMEM((k, GLU_TN), BF16)] * 2,
        compiler_params=_params(("arbitrary", "arbitrary"), win, 2 * _nbytes((k, GLU_TN), BF16)),
        name="swiglu_up",
    )(h, hs, wg, wu)


def _ple_kernel(x_ref, g_ref, p_ref, wg_ref, wp_ref, o_ref, hb):
    j = pl.program_id(1)

    @pl.when(j == 0)
    def _():
        hb[...] = (_rms(x_ref[...]) * g_ref[...]).astype(BF16)

    tn = o_ref.shape[1]
    gate = jax.nn.sigmoid(_dot(hb[...], wg_ref[...]))
    res = x_ref[:, pl.ds(pl.multiple_of(j * tn, tn), tn)]
    o_ref[...] = res + gate * _dot(p_ref[...], wp_ref[...])


def _ple(x, g, p, wg, wp, layer, tm):
    m, k = x.shape
    tn = TN
    win = (_nbytes((tm, k), F32) + _nbytes((tm, PLE_DIM), F32) + _nbytes((k + PLE_DIM, tn), BF16)
           + _nbytes((tm, tn), F32))
    return pl.pallas_call(
        _ple_kernel,
        grid=(m // tm, D_MODEL // tn),
        in_specs=[pl.BlockSpec((tm, k), lambda i, j: (i, 0)),
                  pl.BlockSpec((1, k), lambda i, j: (0, 0)),
                  pl.BlockSpec((None, tm, PLE_DIM), lambda i, j: (layer, i, 0)),
                  _wspec(layer, k, tn, lambda i, j: j),
                  _wspec(layer, PLE_DIM, tn, lambda i, j: j)],
        out_specs=pl.BlockSpec((tm, tn), lambda i, j: (i, j)),
        out_shape=jax.ShapeDtypeStruct((m, D_MODEL), F32),
        scratch_shapes=[pltpu.VMEM((tm, k), BF16)],
        compiler_params=_params(("parallel", "arbitrary"), win, _nbytes((tm, k), BF16)),
        name="ple_gate",
    )(x, g.reshape(1, k), p, wg, wp)


def _sample_mix_kernel(zh_ref, zr_ref, c0_ref, c1_ref, c2_ref, gq_ref, gk_ref, gsgu_ref,
                       w00_ref, b0_ref, bias_ref, bias0_ref,
                       kvn_ref, attn_ref, sguv_ref, sguo_ref, ga_ref, gb_ref, qn_sc, kn_sc):
    nh, hpg = N_ATTN_HEADS, HEADS_PER_GROUP
    qn_sc[...] = (_rms(zh_ref[0, 0:nh, :]) * gq_ref[...]) * Q_SCALE
    kn_sc[...] = _rms(zh_ref[0, nh:2 * nh, :]) * gk_ref[...]

    outs, lses = [], []
    for gi, c_ref in enumerate((c0_ref, c1_ref, c2_ref)):
        hs = slice(gi * hpg, (gi + 1) * hpg)
        q, k_new = qn_sc[hs, :], kn_sc[hs, :]
        v_new = zh_ref[0, 2 * nh + gi * hpg:2 * nh + (gi + 1) * hpg, :]
        kvn_ref[0, gi, 0] = k_new
        kvn_ref[0, gi, 1] = v_new
        kc = c_ref[0, 0, :, 0:hpg, :]
        vc = c_ref[0, 0, :, hpg:2 * hpg, :]
        s = jnp.sum(kc * q[None], axis=-1, keepdims=True) + bias_ref[gi]
        s_new = jnp.sum(q * k_new, axis=-1, keepdims=True) + bias0_ref[gi]
        m = jnp.maximum(jnp.max(s, axis=0), s_new)
        e = jnp.exp(s - m[None])
        e_new = jnp.exp(s_new - m)
        den = jnp.sum(e, axis=0) + e_new
        outs.append((jnp.sum(e * vc, axis=0) + e_new * v_new) / den)
        lses.append(m + jnp.log(den))
    top = jnp.maximum(jnp.maximum(lses[0], lses[1]), lses[2])
    wts = [jnp.exp(l - top) for l in lses]
    attn_ref[0] = (wts[0] * outs[0] + wts[1] * outs[1] + wts[2] * outs[2]) / (wts[0] + wts[1] + wts[2])

    c = GM_WIDTH
    u = _gelu(zr_ref[0, :, 0:c])
    vs = _rms(_gelu(zr_ref[0, :, c:2 * c])) * gsgu_ref[...]
    sguv_ref[0] = vs
    sguo_ref[0] = u * (w00_ref[...] * vs + b0_ref[...])
    ga_ref[0] = jax.nn.sigmoid(zr_ref[0, :, 2 * c:2 * c + D_MODEL])
    gb_ref[0] = jax.nn.sigmoid(zr_ref[0, :, 2 * c + D_MODEL:2 * c + 2 * D_MODEL])


def _sample_mix(z, caches, layer, g_q, g_k, g_sgu, w_sgu, b_sgu, bias_s, bias0_s):
    nb = z.shape[0]
    hpg = HEADS_PER_GROUP
    zh = z[:, :3 * ATTN_WIDTH].reshape(nb, 3 * N_ATTN_HEADS, HEAD_DIM)
    rest = 2 * GM_WIDTH + 2 * D_MODEL
    zr = z[:, 3 * ATTN_WIDTH:].reshape(nb, 1, rest)
    w00 = jnp.repeat(w_sgu[:, 0, 0], BLK).reshape(1, GM_WIDTH)
    b0 = jnp.repeat(b_sgu[:, 0], BLK).reshape(1, GM_WIDTH)
    cache_views, cache_specs = [], []
    for c, (_, dil) in zip(caches, DILATION_GROUPS):
        depth, _, length = c.shape[:3]
        cache_views.append(c.reshape(depth, nb, length // dil, dil * KV_ROWS, HEAD_DIM))
        cache_specs.append(pl.BlockSpec((1, 1, BLK, KV_ROWS, HEAD_DIM), lambda b: (layer, b, 0, 0, 0)))
    vec = lambda width: pl.BlockSpec((1, width), lambda b: (0, 0))
    per_b = lambda width: pl.BlockSpec((1, 1, width), lambda b: (b, 0, 0))
    win = (3 * _nbytes((BLK, KV_ROWS, HEAD_DIM), F32) + _nbytes((1, 4 * rest), F32)
           + _nbytes((N_GROUPS, BLK + 1, hpg, HEAD_DIM), F32))
    return pl.pallas_call(
        _sample_mix_kernel,
        grid=(nb,),
        in_specs=[pl.BlockSpec((1, 3 * N_ATTN_HEADS, HEAD_DIM), lambda b: (b, 0, 0)), per_b(rest),
                  *cache_specs, vec(HEAD_DIM), vec(HEAD_DIM), vec(GM_WIDTH), vec(GM_WIDTH),
                  vec(GM_WIDTH),
                  pl.BlockSpec((N_GROUPS, BLK, hpg, HEAD_DIM), lambda b: (0, 0, 0, 0)),
                  pl.BlockSpec((N_GROUPS, hpg, HEAD_DIM), lambda b: (0, 0, 0))],
        out_specs=[pl.BlockSpec((1, N_GROUPS, 2, hpg, HEAD_DIM), lambda b: (b, 0, 0, 0, 0)),
                   pl.BlockSpec((1, hpg, HEAD_DIM), lambda b: (b, 0, 0)),
                   per_b(GM_WIDTH), per_b(GM_WIDTH), per_b(D_MODEL), per_b(D_MODEL)],
        out_shape=[jax.ShapeDtypeStruct((nb, N_GROUPS, 2, hpg, HEAD_DIM), F32),
                   jax.ShapeDtypeStruct((nb, hpg, HEAD_DIM), F32),
                   jax.ShapeDtypeStruct((nb, 1, GM_WIDTH), F32),
                   jax.ShapeDtypeStruct((nb, 1, GM_WIDTH), F32),
                   jax.ShapeDtypeStruct((nb, 1, D_MODEL), F32),
                   jax.ShapeDtypeStruct((nb, 1, D_MODEL), F32)],
        scratch_shapes=[pltpu.VMEM((N_ATTN_HEADS, HEAD_DIM), F32)] * 2,
        compiler_params=_params(("parallel",), win),
        name="decode_mixers",
    )(zh, zr, *cache_views, g_q.reshape(1, HEAD_DIM), g_k.reshape(1, HEAD_DIM),
      g_sgu.reshape(1, GM_WIDTH), w00, b0, bias_s, bias0_s)


ROLL_TOKENS = 512


def _roll_kernel(chunk_ref, next_ref, new_ref, o_ref):
    kept = chunk_ref.shape[0] - KV_ROWS
    o_ref[0:kept, :] = chunk_ref[KV_ROWS:, :]
    last = pl.program_id(2) == pl.num_programs(2) - 1

    @pl.when(last)
    def _():
        o_ref[kept:, :] = new_ref[...]

    @pl.when(jnp.logical_not(last))
    def _():
        o_ref[kept:, :] = next_ref[...]


def _roll_cache(cache, new_rows):
    depth, nb, length = cache.shape[:3]
    tokens = min(length, ROLL_TOKENS)
    rows = tokens * KV_ROWS
    flat = cache.reshape(depth, nb, length * KV_ROWS, HEAD_DIM)
    token = (None, None, KV_ROWS, HEAD_DIM)
    out = pl.pallas_call(
        _roll_kernel,
        grid=(depth, nb, length // tokens),
        in_specs=[pl.BlockSpec((None, None, rows, HEAD_DIM), lambda d, b, c: (d, b, c, 0)),
                  pl.BlockSpec(token, lambda d, b, c: (d, b, jnp.minimum((c + 1) * tokens, length - 1), 0)),
                  pl.BlockSpec(token, lambda d, b, c: (d, b, 0, 0))],
        out_specs=pl.BlockSpec((None, None, rows, HEAD_DIM), lambda d, b, c: (d, b, c, 0)),
        out_shape=jax.ShapeDtypeStruct(flat.shape, cache.dtype),
        compiler_params=_params(("parallel", "parallel", "arbitrary"), 2 * _nbytes((rows, HEAD_DIM), F32)),
        name="roll_cache",
    )(flat, flat, new_rows.reshape(depth, nb, KV_ROWS, HEAD_DIM))
    return out.reshape(cache.shape)


def _t5_bucket(dist):
    dist = np.asarray(dist)
    max_exact = NUM_BUCKETS // 2
    large = max_exact + (np.log(np.maximum(dist, 1) / max_exact) / np.log(REL_MAX_DIST / max_exact)
                         * (NUM_BUCKETS - max_exact)).astype(np.int32)
    large = np.minimum(large, NUM_BUCKETS - 1)
    return np.where(dist < max_exact, dist, large).astype(np.int32)


def _bias_tables(rel_bias):
    hpg = HEADS_PER_GROUP
    prompt, dec, dec0 = [], [], []
    for gi, (_, dil) in enumerate(DILATION_GROUPS):
        heads = rel_bias[:, gi * hpg:(gi + 1) * hpg].astype(F32)
        by_off = jnp.take(heads, _t5_bucket(np.arange(BLK + 1) * dil), axis=0).T
        w = jnp.concatenate([by_off[:, ::-1], jnp.full((hpg, BLK), NEG_INF, F32)], axis=1)
        tab = jnp.tile(w, (1, BLK))[:, :BLK * 2 * BLK].reshape(hpg, BLK, 2 * BLK)
        prompt.append(tab)
        dec.append(jnp.broadcast_to(by_off[:, :0:-1].T[:, :, None], (BLK, hpg, HEAD_DIM)))
        dec0.append(jnp.broadcast_to(by_off[:, 0:1], (hpg, HEAD_DIM)))
    return jnp.stack(prompt), jnp.stack(dec), jnp.stack(dec0)


def _prepare_weights(w_in, w_attn_br, w_sgu_br, w_out, w_ffn_gate, w_ffn_up, w_ffn_down,
                     w_ple_gate, w_ple_proj):
    return dict(
        w_in=w_in, w_gate=w_ffn_gate, w_up=w_ffn_up,
        w_kv=w_in[:, :, ATTN_WIDTH:3 * ATTN_WIDTH].astype(BF16),
        w_attn_br=w_attn_br.astype(BF16), w_sgu_br=w_sgu_br.astype(BF16), w_out=w_out.astype(BF16),
        w_down=w_ffn_down.astype(BF16),
        w_ple_gate=w_ple_gate.astype(BF16), w_ple_proj=w_ple_proj.astype(BF16))


DOWN_TK = 2816


def _layer(xp, xs, pp, ps, caches, wts, layer, depth, kv_prev, tables, g_mix, g_q, g_k, g_sgu, w_sgu,
           b_sgu, g_ffn, g_ple, batch, seq):
    tm, nb = 1024, xs.shape[0]
    bias_tab, bias_s, bias0_s = tables
    w_in = wts["w_in"]

    h = _rmsnorm(xp, g_mix, BF16, 256)
    hs = _rmsnorm(xs, g_mix, F32, nb)
    q_hm, zq = _q_proj(h, hs, w_in, layer, g_q, batch, seq, RES_TM)
    uv, zu = _proj_act(h, hs, w_in, layer, COL_U * TN // PTN, 2 * GM_WIDTH // PTN, "gelu", RES_TM)
    gates, zg = _proj_act(h, hs, w_in, layer, COL_GATE * TN // PTN, 2 * D_MODEL // PTN, "sigmoid", RES_TM)
    ks, vs, kvs = [], [], []
    for gi in range(N_GROUPS):
        k_hm, v_hm, kv = _kv_proj(h, wts["w_kv"], layer, depth, g_k, gi, kv_prev[gi], batch, seq, 512)
        ks.append(k_hm)
        vs.append(v_hm)
        kvs.append(kv)
    zkv = _proj_rows(hs, wts["w_kv"], layer, 2 * N_GROUPS)

    attn_p = _attention(q_hm, ks, vs, bias_tab, batch, seq).reshape(batch * seq, ATTN_OUT)
    sgu_p = _sgu(uv, g_sgu, w_sgu, b_sgu, 512)
    z = jnp.concatenate([zq, zkv, zu, zg], axis=1)
    kvn, attn_s, sguv, sgu_s, ga_s, gb_s = _sample_mix(z, caches, layer, g_q, g_k, g_sgu, w_sgu, b_sgu,
                                                       bias_s, bias0_s)

    wa, wb = wts["w_attn_br"], wts["w_sgu_br"]
    merged_p = _merge(attn_p, sgu_p, wa, wb, layer, gates, gates, D_MODEL // TN, BF16, tm)
    merged_s = _merge(attn_s.reshape(nb, ATTN_OUT), sgu_s.reshape(nb, GM_WIDTH), wa, wb, layer,
                      ga_s.reshape(nb, D_MODEL), gb_s.reshape(nb, D_MODEL), 0, F32, nb)
    xp = _mm_res(merged_p, wts["w_out"], layer, xp, tm, D_MODEL)
    xs = _mm_res(merged_s, wts["w_out"], layer, xs, nb, D_MODEL)

    ff_p, ff_s = _glu(_rmsnorm(xp, g_ffn, BF16, 256), _rmsnorm(xs, g_ffn, F32, nb),
                      wts["w_gate"], wts["w_up"], layer, RES_TM)
    xp = _mm_res(ff_p, wts["w_down"], layer, xp, tm, DOWN_TK)
    xs = _mm_res(ff_s, wts["w_down"], layer, xs, nb, DOWN_TK)

    xp = _ple(xp, g_ple, pp, wts["w_ple_gate"], wts["w_ple_proj"], layer, RES_TM)
    xs = _ple(xs, g_ple, ps, wts["w_ple_gate"], wts["w_ple_proj"], layer, nb)
    return xp, xs, kvs, kvn, sguv


def kernel(x_prompt, x_sample, cache_kv_w128, cache_kv_w512, cache_kv_w2048, p_prompt, p_sample,
           rel_bias, g_mix, w_in, g_q, g_k, g_sgu, w_sgu, b_sgu, w_attn_br, w_sgu_br, w_out, g_ffn,
           w_ffn_gate, w_ffn_up, w_ffn_down, g_ple, w_ple_gate, w_ple_proj):
    batch, seq, _ = x_prompt.shape
    nb = x_sample.shape[0]
    depth = w_in.shape[0]
    caches = (cache_kv_w128, cache_kv_w512, cache_kv_w2048)
    tables = _bias_tables(rel_bias)
    wts = _prepare_weights(w_in, w_attn_br, w_sgu_br, w_out, w_ffn_gate, w_ffn_up, w_ffn_down,
                           w_ple_gate, w_ple_proj)

    xp = x_prompt.reshape(batch * seq, D_MODEL)
    xs = x_sample.reshape(nb, D_MODEL)
    pp = p_prompt.reshape(depth, batch * seq, PLE_DIM)
    ps = p_sample.reshape(depth, nb, PLE_DIM)
    kv_p = [None] * N_GROUPS
    kv_new = [[] for _ in range(N_GROUPS)]
    sgu_v = []
    for i in range(depth):
        norms = (g_mix[i], g_q[i], g_k[i], g_sgu[i], w_sgu[i], b_sgu[i], g_ffn[i], g_ple[i])
        xp, xs, kv_p, kvn, sguv = _layer(xp, xs, pp, ps, caches, wts, i, depth, kv_p, tables, *norms,
                                         batch, seq)
        for gi in range(N_GROUPS):
            kv_new[gi].append(kvn[:, gi])
        sgu_v.append(sguv)

    kv_p = [kv.reshape(depth, batch, -1, 2, HEADS_PER_GROUP, HEAD_DIM) for kv in kv_p]
    kv_s = [_roll_cache(c, jnp.stack(kv_new[gi], axis=0)) for gi, c in enumerate(caches)]
    return (xp.reshape(batch, seq, D_MODEL), xs.reshape(nb, 1, D_MODEL), *kv_p, *kv_s,
            jnp.stack(sgu_v, axis=0))
```

```python
import functools

import numpy as np
import jax
import jax.numpy as jnp
from jax import lax
from jax.experimental import pallas as pl
from jax.experimental.pallas import tpu as pltpu

D_MODEL = 4096
HEAD_DIM = 128
DILATION_GROUPS = ((128, 1), (512, 4), (2048, 16))
N_GROUPS = 3
HEADS_PER_GROUP = 8
N_ATTN_HEADS = 24
ATTN_WIDTH = 3072
ATTN_OUT = 1024
BLK = 128
KV_ROWS = 2 * HEADS_PER_GROUP
GM_WIDTH = 2048
GM_GROUPS = 16
PLE_DIM = 256
NUM_BUCKETS = 32
REL_MAX_DIST = 2048
NORM_EPS = 1e-6
NEG_INF = -1e30
Q_SCALE = HEAD_DIM ** -0.5

TN = 1024
COL_U, COL_GATE = 9, 13

V7X_VMEM_REQUEST_CAP = 60000 * 1024
COMPILER_TEMP_BYTES = 16 * 1024 * 1024

F32 = jnp.float32
BF16 = jnp.bfloat16


def _params(sems, window_bytes, scratch_bytes=0):
    limit = min(V7X_VMEM_REQUEST_CAP, 2 * window_bytes + scratch_bytes + COMPILER_TEMP_BYTES)
    return pltpu.CompilerParams(dimension_semantics=sems, vmem_limit_bytes=int(limit))


def _nbytes(shape, dtype):
    return int(np.prod(shape)) * jnp.dtype(dtype).itemsize


def _gelu(x):
    return x * (0.5 * (1.0 + jnp.tanh(0.7978845608028654 * (x + 0.044715 * (x * x * x)))))


def _silu(x):
    return x * jax.nn.sigmoid(x)


_ACT = {"gelu": _gelu, "sigmoid": jax.nn.sigmoid, "none": lambda x: x}


def _rms(x):
    return x * lax.rsqrt(jnp.mean(x * x, axis=-1, keepdims=True) + NORM_EPS)


def _dot(a, b):
    return jnp.dot(a.astype(BF16), b.astype(BF16), preferred_element_type=F32)


def _dot_nt(a, b):
    return lax.dot_general(a, b, (((1,), (1,)), ((), ())), preferred_element_type=F32)


def _wspec(layer, k, tn, col_map):
    return pl.BlockSpec((None, k, tn), lambda *ids: (layer, 0, col_map(*ids)))


def _rmsnorm_kernel(x_ref, g_ref, o_ref):
    o_ref[...] = (_rms(x_ref[...]) * g_ref[...]).astype(o_ref.dtype)


def _rmsnorm(x, g, out_dtype, tm):
    m, d = x.shape
    return pl.pallas_call(
        _rmsnorm_kernel,
        grid=(m // tm,),
        in_specs=[pl.BlockSpec((tm, d), lambda i: (i, 0)), pl.BlockSpec((1, d), lambda i: (0, 0))],
        out_specs=pl.BlockSpec((tm, d), lambda i: (i, 0)),
        out_shape=jax.ShapeDtypeStruct((m, d), out_dtype),
        compiler_params=_params(("parallel",), _nbytes((tm, d), F32) + _nbytes((tm, d), out_dtype)),
        name="rmsnorm",
    )(x, g.reshape(1, d))


def _proj_rows_kernel(x_ref, inv_ref, w_ref, o_ref):
    o_ref[...] = _dot(x_ref[...], w_ref[...]) * inv_ref[...]


def _proj_rows(h, inv, w, layer, n_tiles):
    m, k = h.shape
    win = _nbytes((m, k), h.dtype) + _nbytes((k, TN), BF16) + _nbytes((m, TN), F32)
    return pl.pallas_call(
        _proj_rows_kernel,
        grid=(n_tiles,),
        in_specs=[pl.BlockSpec((m, k), lambda j: (0, 0)), pl.BlockSpec((m, 1), lambda j: (0, 0)),
                  _wspec(layer, k, TN, lambda j: j)],
        out_specs=pl.BlockSpec((m, TN), lambda j: (0, j)),
        out_shape=jax.ShapeDtypeStruct((m, n_tiles * TN), F32),
        compiler_params=_params(("parallel",), win),
        name="proj_rows",
    )(h, inv, w)


PTN = 1024
RES_TM = 512


def _resident_cast(w_refs, wb_refs):
    for w_ref, wb_ref in zip(w_refs, wb_refs):
        wb_ref[...] = w_ref[...].astype(BF16)


def _row_specs(tm, k, nb):
    return [pl.BlockSpec((tm, k), lambda j, i: (i, 0)), pl.BlockSpec((tm, 1), lambda j, i: (i, 0)),
            pl.BlockSpec((nb, k), lambda j, i: (0, 0)), pl.BlockSpec((nb, 1), lambda j, i: (0, 0))]


def _proj_act_kernel(x_ref, inv_ref, xs_ref, invs_ref, w_ref, o_ref, os_ref, wb, *, act):
    @pl.when(pl.program_id(1) == 0)
    def _():
        _resident_cast([w_ref], [wb])
        os_ref[...] = _dot(xs_ref[...], wb[...]) * invs_ref[...]

    o_ref[...] = _ACT[act](_dot(x_ref[...], wb[...]) * inv_ref[...]).astype(o_ref.dtype)


def _proj_act(h, inv, hs, invs, w, layer, col0, n_tiles, act, tm):
    (m, k), nb = h.shape, hs.shape[0]
    win = _nbytes((tm, k), BF16) + _nbytes((k, PTN), F32) + _nbytes((tm, PTN), BF16)
    return pl.pallas_call(
        functools.partial(_proj_act_kernel, act=act),
        grid=(n_tiles, m // tm),
        in_specs=[*_row_specs(tm, k, nb), _wspec(layer, k, PTN, lambda j, i: j + col0)],
        out_specs=[pl.BlockSpec((tm, PTN), lambda j, i: (i, j)),
                   pl.BlockSpec((nb, PTN), lambda j, i: (0, j))],
        out_shape=[jax.ShapeDtypeStruct((m, n_tiles * PTN), BF16),
                   jax.ShapeDtypeStruct((nb, n_tiles * PTN), F32)],
        scratch_shapes=[pltpu.VMEM((k, PTN), BF16)],
        compiler_params=_params(("arbitrary", "arbitrary"), win, _nbytes((k, PTN), BF16)),
        name="proj_" + act,
    )(h, inv, hs, invs, w)


def _q_kernel(x_ref, inv_ref, xs_ref, invs_ref, w_ref, g_ref, o_ref, os_ref, wb):
    @pl.when(pl.program_id(1) == 0)
    def _():
        _resident_cast([w_ref], [wb])
        os_ref[...] = _dot(xs_ref[...], wb[...]) * invs_ref[...]

    acc = _dot(x_ref[...], wb[...]) * inv_ref[...]
    for hh in range(PTN // HEAD_DIM):
        a = acc[:, hh * HEAD_DIM:(hh + 1) * HEAD_DIM]
        o_ref[0, hh] = ((_rms(a) * g_ref[...]) * Q_SCALE).astype(o_ref.dtype)


def _q_proj(h, inv, hs, invs, w, layer, g_q, batch, seq, tm):
    (m, k), nb = h.shape, hs.shape[0]
    tpb = seq // tm
    heads = PTN // HEAD_DIM
    win = _nbytes((tm, k), BF16) + _nbytes((k, PTN), F32) + _nbytes((tm, PTN), BF16)
    return pl.pallas_call(
        _q_kernel,
        grid=(ATTN_WIDTH // PTN, m // tm),
        in_specs=[*_row_specs(tm, k, nb), _wspec(layer, k, PTN, lambda j, i: j),
                  pl.BlockSpec((1, HEAD_DIM), lambda j, i: (0, 0))],
        out_specs=[pl.BlockSpec((1, heads, tm, HEAD_DIM), lambda j, i: (i // tpb, j, i % tpb, 0)),
                   pl.BlockSpec((nb, PTN), lambda j, i: (0, j))],
        out_shape=[jax.ShapeDtypeStruct((batch, N_ATTN_HEADS, seq, HEAD_DIM), BF16),
                   jax.ShapeDtypeStruct((nb, ATTN_WIDTH), F32)],
        scratch_shapes=[pltpu.VMEM((k, PTN), BF16)],
        compiler_params=_params(("arbitrary", "arbitrary"), win, _nbytes((k, PTN), BF16)),
        name="q_proj",
    )(h, inv, hs, invs, w, g_q.reshape(1, HEAD_DIM))


def _kv_kernel(x_ref, inv_ref, wk_ref, wv_ref, g_ref, kv_prev_ref, khm_ref, vhm_ref, kv_ref, *, rows):
    del kv_prev_ref
    x, inv = x_ref[...], inv_ref[...]
    k = _dot(x, wk_ref[...]) * inv
    v = _dot(x, wv_ref[...]) * inv
    tm = x.shape[0]
    for hh in range(HEADS_PER_GROUP):
        sl = slice(hh * HEAD_DIM, (hh + 1) * HEAD_DIM)
        kn = _rms(k[:, sl]) * g_ref[...]
        vh = v[:, sl]
        khm_ref[0, hh] = kn.astype(BF16)
        vhm_ref[0, hh] = vh.astype(BF16)
        kv_ref[pl.ds(hh, rows, stride=KV_ROWS), :] = kn[tm - rows:]
        kv_ref[pl.ds(HEADS_PER_GROUP + hh, rows, stride=KV_ROWS), :] = vh[tm - rows:]


def _kv_proj(h, inv, w, layer, depth, g_k, gi, kv_prev, batch, seq, tm):
    m, k = h.shape
    tpb = seq // tm
    keep = min(DILATION_GROUPS[gi][0], seq)
    rows = min(keep, tm)
    kv_tiles = keep // rows
    kv_map = lambda i: (layer, i // tpb, jnp.maximum(i % tpb - (tpb - kv_tiles), 0), 0)
    hm_spec = pl.BlockSpec((1, HEADS_PER_GROUP, tm, HEAD_DIM), lambda i: (i // tpb, 0, i % tpb, 0))
    hm_shape = jax.ShapeDtypeStruct((batch, HEADS_PER_GROUP, seq, HEAD_DIM), BF16)
    kv_shape = jax.ShapeDtypeStruct((depth, batch, keep * KV_ROWS, HEAD_DIM), F32)
    win = (_nbytes((tm, k), BF16) + _nbytes((k, TN), BF16) + 2 * _nbytes((tm, TN), BF16)
           + _nbytes((rows * KV_ROWS, HEAD_DIM), F32))
    if kv_prev is None:
        kv_prev = jnp.zeros(kv_shape.shape, F32)
    return pl.pallas_call(
        functools.partial(_kv_kernel, rows=rows),
        grid=(m // tm,),
        in_specs=[pl.BlockSpec((tm, k), lambda i: (i, 0)), pl.BlockSpec((tm, 1), lambda i: (i, 0)),
                  pl.BlockSpec((None, k, TN), lambda i: (layer, 0, gi), pipeline_mode=pl.Buffered(1)),
                  pl.BlockSpec((None, k, TN), lambda i: (layer, 0, N_GROUPS + gi),
                               pipeline_mode=pl.Buffered(1)),
                  pl.BlockSpec((1, HEAD_DIM), lambda i: (0, 0)),
                  pl.BlockSpec(memory_space=pl.ANY)],
        out_specs=[hm_spec, hm_spec, pl.BlockSpec((None, None, rows * KV_ROWS, HEAD_DIM), kv_map)],
        out_shape=[hm_shape, hm_shape, kv_shape],
        input_output_aliases={5: 2},
        compiler_params=_params(("arbitrary",), win),
        name="kv_proj",
    )(h, inv, w, w, g_k.reshape(1, HEAD_DIM), kv_prev)


ATTN_BATCH = 4


def _attn_blocks(qkv, bias):
    scores = [_dot_nt(q, k) + bias for q, k, _ in qkv]
    tops = [jnp.max(s, axis=-1, keepdims=True) for s in scores]
    probs = [jnp.exp(s - m).astype(BF16) for s, m in zip(scores, tops)]
    nds = [jnp.dot(e, jnp.concatenate([v, jnp.ones_like(v)], axis=1), preferred_element_type=F32)
           for e, (_, _, v) in zip(probs, qkv)]
    return [(nd[:, :HEAD_DIM] / nd[:, HEAD_DIM:], m + jnp.log(nd[:, HEAD_DIM:])) for nd, m in zip(nds, tops)]


def _attn_kernel(q0_ref, q1_ref, q2_ref, k0_ref, k1_ref, k2_ref, v0_ref, v1_ref, v2_ref, bias_ref, o_ref,
                 q1f, q2f, k1f, k2f, v1f, v2f, o0_sc, o1_sc, o2_sc, l0_sc, l1_sc, l2_sc, *, seq):
    for src, dst in ((q1_ref, q1f), (q2_ref, q2f), (k1_ref, k1f), (k2_ref, k2f), (v1_ref, v1f), (v2_ref, v2f)):
        dst[...] = src[0, 0].astype(F32)

    def run_group(gi, load, o_sc, l_sc, unroll):
        dil = DILATION_GROUPS[gi][1]
        nblk = seq // dil // BLK

        def rows(r, s0, n):
            if dil == 1:
                return pl.ds(s0 if isinstance(s0, int) else pl.multiple_of(s0, BLK), n)
            return pl.ds(s0 * dil + r, n, stride=dil)

        def blocks(todo, first):
            back = 0 if first else BLK
            bias = bias_ref[gi, 0, :, BLK:2 * BLK] if first else bias_ref[gi, 0]
            qkv = [load(rows(r, s0, BLK), rows(r, s0 - back, BLK + back)) for r, s0 in todo]
            for (r, s0), (o, lse) in zip(todo, _attn_blocks(qkv, bias)):
                o_sc[rows(r, s0, BLK), :] = o
                l_sc[rows(r, s0, BLK), :] = lse

        for r0 in range(0, dil, ATTN_BATCH):
            blocks([(r, 0) for r in range(r0, min(dil, r0 + ATTN_BATCH))], True)

        def body(it, carry):
            blocks([(r, (1 + it * unroll + u) * BLK) for u in range(unroll) for r in range(dil)], False)
            return carry

        if nblk > 1:
            lax.fori_loop(0, (nblk - 1) // unroll, body, 0)

    def bf16_load(q_ref, k_ref, v_ref):
        return lambda qr, kr: (q_ref[0, 0, qr, :], k_ref[0, 0, kr, :], v_ref[0, 0, kr, :])

    def f32_load(qf, kf, vf):
        return lambda qr, kr: (qf[qr, :].astype(BF16), kf[kr, :].astype(BF16), vf[kr, :].astype(BF16))

    run_group(0, bf16_load(q0_ref, k0_ref, v0_ref), o0_sc, l0_sc, 5)
    run_group(1, f32_load(q1f, k1f, v1f), o1_sc, l1_sc, 1)
    run_group(2, f32_load(q2f, k2f, v2f), o2_sc, l2_sc, 1)

    def combine(c, carry):
        rows = pl.ds(pl.multiple_of(c * BLK, BLK), BLK)
        l0, l1, l2 = l0_sc[rows, :], l1_sc[rows, :], l2_sc[rows, :]
        top = jnp.maximum(jnp.maximum(l0, l1), l2)
        w0, w1, w2 = jnp.exp(l0 - top), jnp.exp(l1 - top), jnp.exp(l2 - top)
        mixed = (w0 * o0_sc[rows, :] + w1 * o1_sc[rows, :] + w2 * o2_sc[rows, :]) / (w0 + w1 + w2)
        o_ref[0, rows, :] = mixed.astype(o_ref.dtype)
        return carry

    lax.fori_loop(0, seq // BLK, combine, 0)


def _attention(q_hm, ks, vs, bias_tab, batch, seq):
    slab = (1, 1, seq, HEAD_DIM)
    specs = [pl.BlockSpec(slab, lambda b, h, gi=gi: (b, gi * HEADS_PER_GROUP + h, 0, 0))
             for gi in range(N_GROUPS)]
    specs += [pl.BlockSpec(slab, lambda b, h: (b, h, 0, 0))] * (2 * N_GROUPS)
    specs.append(pl.BlockSpec((N_GROUPS, 1, BLK, 2 * BLK), lambda b, h: (0, h, 0, 0)))
    win = (10 * _nbytes((seq, HEAD_DIM), BF16) + _nbytes((N_GROUPS, BLK, 2 * BLK), F32)
           + 6 * _nbytes((seq, HEAD_DIM), F32))
    return pl.pallas_call(
        functools.partial(_attn_kernel, seq=seq),
        grid=(batch, HEADS_PER_GROUP),
        in_specs=specs,
        out_specs=pl.BlockSpec((1, seq, HEAD_DIM), lambda b, h: (b, 0, h)),
        out_shape=jax.ShapeDtypeStruct((batch, seq, ATTN_OUT), BF16),
        scratch_shapes=[pltpu.VMEM((seq, HEAD_DIM), F32)] * 12,
        compiler_params=_params(("parallel", "arbitrary"), win),
        name="dilated_attention",
    )(q_hm, q_hm, q_hm, *ks, *vs, bias_tab)


def _sgu_kernel(u_ref, gv_ref, g_ref, w_ref, bt_ref, o_ref, vs_sc):
    gv = gv_ref[...].astype(F32)
    vs_sc[...] = (_rms(gv) * g_ref[...]).astype(BF16)
    tm = gv.shape[0]
    row = lax.broadcasted_iota(jnp.int32, (BLK, BLK), 0)
    col = lax.broadcasted_iota(jnp.int32, (BLK, BLK), 1)
    tril = (row >= col).astype(F32)
    for g in range(GM_GROUPS):
        cols = slice(g * BLK, (g + 1) * BLK)
        wg = (w_ref[g] * tril).astype(BF16)
        bg = bt_ref[:, g:g + 1]
        for c in range(tm // BLK):
            rows = slice(c * BLK, (c + 1) * BLK)
            mix = _dot(wg, vs_sc[rows, cols]) + bg
            o_ref[rows, cols] = (u_ref[rows, cols].astype(F32) * mix).astype(o_ref.dtype)


def _sgu(uv, g_sgu, w_sgu, b_sgu, tm):
    m, c = uv.shape[0], GM_WIDTH
    win = 3 * _nbytes((tm, c), BF16) + _nbytes((GM_GROUPS, BLK, BLK), F32)
    return pl.pallas_call(
        _sgu_kernel,
        grid=(m // tm,),
        in_specs=[pl.BlockSpec((tm, c), lambda i: (i, 0)),
                  pl.BlockSpec((tm, c), lambda i: (i, 1)),
                  pl.BlockSpec((1, c), lambda i: (0, 0)),
                  pl.BlockSpec((GM_GROUPS, BLK, BLK), lambda i: (0, 0, 0)),
                  pl.BlockSpec((BLK, GM_GROUPS), lambda i: (0, 0))],
        out_specs=pl.BlockSpec((tm, c), lambda i: (i, 0)),
        out_shape=jax.ShapeDtypeStruct((m, c), BF16),
        scratch_shapes=[pltpu.VMEM((tm, c), BF16)],
        compiler_params=_params(("parallel",), win),
        name="spatial_gating",
    )(uv, uv, g_sgu.reshape(1, c), w_sgu, b_sgu.T)


def _merge_kernel(a_ref, s_ref, wa_ref, wb_ref, ga_ref, gb_ref, o_ref):
    a = _dot(a_ref[...], wa_ref[...])
    b = _dot(s_ref[...], wb_ref[...])
    o_ref[...] = (ga_ref[...].astype(F32) * a + gb_ref[...].astype(F32) * b).astype(o_ref.dtype)


def _merge(attn, sgu, wa, wb, layer, ga, gb, gb_col0, out_dtype, tm):
    m = attn.shape[0]
    win = (_nbytes((tm, ATTN_OUT), attn.dtype) + _nbytes((tm, GM_WIDTH), sgu.dtype)
           + _nbytes((ATTN_OUT + GM_WIDTH, TN), BF16) + 2 * _nbytes((tm, TN), ga.dtype)
           + _nbytes((tm, TN), out_dtype))
    row = lambda width: pl.BlockSpec((tm, width), lambda i, j: (i, 0))
    tile = pl.BlockSpec((tm, TN), lambda i, j: (i, j))
    return pl.pallas_call(
        _merge_kernel,
        grid=(m // tm, D_MODEL // TN),
        in_specs=[row(ATTN_OUT), row(GM_WIDTH),
                  _wspec(layer, ATTN_OUT, TN, lambda i, j: j),
                  _wspec(layer, GM_WIDTH, TN, lambda i, j: j), tile,
                  pl.BlockSpec((tm, TN), lambda i, j: (i, j + gb_col0))],
        out_specs=tile,
        out_shape=jax.ShapeDtypeStruct((m, D_MODEL), out_dtype),
        compiler_params=_params(("parallel", "arbitrary"), win),
        name="branch_merge",
    )(attn, sgu, wa, wb, ga, gb)


def _emit_scaled(y, g_ref, xg_ref, inv_ref, ss_sc, j, n_col_tiles):
    xg_ref[...] = (y * g_ref[...]).astype(xg_ref.dtype)
    ss = jnp.sum(y * y, axis=-1, keepdims=True)

    @pl.when(j == 0)
    def _():
        ss_sc[...] = ss

    @pl.when(j > 0)
    def _():
        ss_sc[...] += ss

    @pl.when(j == n_col_tiles - 1)
    def _():
        inv_ref[...] = lax.rsqrt(ss_sc[...] * (1.0 / (n_col_tiles * y.shape[1])) + NORM_EPS)


def _norm_out(m, n, tm, tn, xg_dtype, col_of):
    in_spec = pl.BlockSpec((1, tn), lambda *ids: (0, col_of(*ids)))
    out_specs = [pl.BlockSpec((tm, tn), lambda *ids: (ids[0], col_of(*ids))),
                 pl.BlockSpec((tm, 1), lambda *ids: (ids[0], 0))]
    out_shape = [jax.ShapeDtypeStruct((m, n), xg_dtype), jax.ShapeDtypeStruct((m, 1), F32)]
    return in_spec, out_specs, out_shape, pltpu.VMEM((tm, 1), F32)


def _mm_res_kernel(x_ref, w_ref, r_ref, *rest, k_rows, norm, grid):
    o_ref = rest[1] if norm else rest[0]
    x, w = x_ref[...], w_ref[...]
    tk = w.shape[0]
    if k_rows % tk:
        k0 = pl.program_id(2) * tk
        w = jnp.where(k0 + lax.broadcasted_iota(jnp.int32, w.shape, 0) < k_rows, w, jnp.zeros_like(w))
        x = jnp.where(k0 + lax.broadcasted_iota(jnp.int32, x.shape, 1) < k_rows, x, jnp.zeros_like(x))
    part = _dot(x, w)

    @pl.when(pl.program_id(2) == 0)
    def _():
        o_ref[...] = r_ref[...] + part

    @pl.when(pl.program_id(2) > 0)
    def _():
        o_ref[...] += part

    if norm:
        g_ref, _, xg_ref, inv_ref, ss_sc = rest

        @pl.when(pl.program_id(2) == grid[2] - 1)
        def _():
            _emit_scaled(o_ref[...], g_ref, xg_ref, inv_ref, ss_sc, pl.program_id(1), grid[1])


def _mm_res(x, w, layer, res, tm, tk, g_next=None, xg_dtype=None):
    m, k = x.shape
    n = w.shape[2]
    norm = g_next is not None
    win = _nbytes((tm, tk), x.dtype) + _nbytes((tk, TN), BF16) + 2 * _nbytes((tm, TN), F32)
    ins = [x, w, res]
    in_specs = [pl.BlockSpec((tm, tk), lambda i, j, kk: (i, kk)),
                pl.BlockSpec((None, tk, TN), lambda i, j, kk: (layer, kk, j)),
                pl.BlockSpec((tm, TN), lambda i, j, kk: (i, j))]
    out_specs = [pl.BlockSpec((tm, TN), lambda i, j, kk: (i, j))]
    out_shape = [jax.ShapeDtypeStruct((m, n), F32)]
    scratch = []
    if norm:
        g_spec, n_specs, n_shapes, ss = _norm_out(m, n, tm, TN, xg_dtype, lambda i, j, kk: j)
        ins.append(g_next.reshape(1, n))
        in_specs.append(g_spec)
        out_specs += n_specs
        out_shape += n_shapes
        scratch.append(ss)
        win += _nbytes((tm, TN), xg_dtype)
    grid = (m // tm, n // TN, pl.cdiv(k, tk))
    out = pl.pallas_call(
        functools.partial(_mm_res_kernel, k_rows=k, norm=norm, grid=grid),
        grid=grid,
        in_specs=in_specs,
        out_specs=out_specs,
        out_shape=out_shape,
        scratch_shapes=scratch,
        compiler_params=_params(("parallel", "arbitrary", "arbitrary"), win),
        name="matmul_residual",
    )(*ins)
    return out if norm else out[0]


GLU_TN = 256


def _glu_kernel(x_ref, inv_ref, xs_ref, invs_ref, wg_ref, wu_ref, o_ref, os_ref, wbg, wbu):
    @pl.when(pl.program_id(1) == 0)
    def _():
        _resident_cast([wg_ref, wu_ref], [wbg, wbu])
        xs, invs = xs_ref[...], invs_ref[...]
        os_ref[...] = _silu(_dot(xs, wbg[...]) * invs) * (_dot(xs, wbu[...]) * invs)

    x, inv = x_ref[...], inv_ref[...]
    o_ref[...] = (_silu(_dot(x, wbg[...]) * inv) * (_dot(x, wbu[...]) * inv)).astype(o_ref.dtype)


def _glu(h, inv, hs, invs, wg, wu, layer, tm):
    (m, k), nb = h.shape, hs.shape[0]
    n = wg.shape[2]
    win = _nbytes((tm, k), BF16) + 2 * _nbytes((k, GLU_TN), F32) + _nbytes((tm, GLU_TN), BF16)
    wspec = _wspec(layer, k, GLU_TN, lambda j, i: j)
    return pl.pallas_call(
        _glu_kernel,
        grid=(n // GLU_TN, m // tm),
        in_specs=[*_row_specs(tm, k, nb), wspec, wspec],
        out_specs=[pl.BlockSpec((tm, GLU_TN), lambda j, i: (i, j)),
                   pl.BlockSpec((nb, GLU_TN), lambda j, i: (0, j))],
        out_shape=[jax.ShapeDtypeStruct((m, n), BF16), jax.ShapeDtypeStruct((nb, n), F32)],
        scratch_shapes=[pltpu.VMEM((k, GLU_TN), BF16)] * 2,
        compiler_params=_params(("arbitrary", "arbitrary"), win, 2 * _nbytes((k, GLU_TN), BF16)),
        name="swiglu_up",
    )(h, inv, hs, invs, wg, wu)


PLE_TN = 512


def _ple_kernel(h_ref, inv_ref, p_ref, wg_ref, wp_ref, r_ref, *rest, norm, grid):
    o_ref = rest[1] if norm else rest[0]
    gate = jax.nn.sigmoid(_dot(h_ref[...], wg_ref[...]) * inv_ref[...])
    y = r_ref[...] + gate * _dot(p_ref[...], wp_ref[...])
    o_ref[...] = y
    if norm:
        g_ref, _, xg_ref, invn_ref, ss_sc = rest
        _emit_scaled(y, g_ref, xg_ref, invn_ref, ss_sc, pl.program_id(1), grid[1])


def _ple(h, inv, p, wg, wp, layer, res, tm, g_next=None, xg_dtype=None):
    m, k = h.shape
    tn = PLE_TN
    norm = g_next is not None
    win = (_nbytes((tm, k), h.dtype) + _nbytes((tm, PLE_DIM), F32) + _nbytes((k + PLE_DIM, tn), BF16)
           + 2 * _nbytes((tm, tn), F32))
    tile = pl.BlockSpec((tm, tn), lambda i, j: (i, j))
    ins = [h, inv, p, wg, wp, res]
    in_specs = [pl.BlockSpec((tm, k), lambda i, j: (i, 0)),
                pl.BlockSpec((tm, 1), lambda i, j: (i, 0)),
                pl.BlockSpec((None, tm, PLE_DIM), lambda i, j: (layer, i, 0)),
                _wspec(layer, k, tn, lambda i, j: j),
                _wspec(layer, PLE_DIM, tn, lambda i, j: j), tile]
    out_specs = [tile]
    out_shape = [jax.ShapeDtypeStruct((m, D_MODEL), F32)]
    scratch = []
    if norm:
        g_spec, n_specs, n_shapes, ss = _norm_out(m, D_MODEL, tm, tn, xg_dtype, lambda i, j: j)
        ins.append(g_next.reshape(1, D_MODEL))
        in_specs.append(g_spec)
        out_specs += n_specs
        out_shape += n_shapes
        scratch.append(ss)
        win += _nbytes((tm, tn), xg_dtype)
    grid = (m // tm, D_MODEL // tn)
    out = pl.pallas_call(
        functools.partial(_ple_kernel, norm=norm, grid=grid),
        grid=grid,
        in_specs=in_specs,
        out_specs=out_specs,
        out_shape=out_shape,
        scratch_shapes=scratch,
        compiler_params=_params(("parallel", "arbitrary"), win),
        name="ple_gate",
    )(*ins)
    return out if norm else out[0]


def _sample_mix_kernel(zh_ref, zr_ref, c0_ref, c1_ref, c2_ref, gq_ref, gk_ref, gsgu_ref,
                       w00_ref, b0_ref, bias_ref, bias0_ref,
                       kvn_ref, attn_ref, sguv_ref, sguo_ref, ga_ref, gb_ref, qn_sc, kn_sc):
    nh, hpg = N_ATTN_HEADS, HEADS_PER_GROUP
    qn_sc[...] = (_rms(zh_ref[0, 0:nh, :]) * gq_ref[...]) * Q_SCALE
    kn_sc[...] = _rms(zh_ref[0, nh:2 * nh, :]) * gk_ref[...]

    outs, lses = [], []
    for gi, c_ref in enumerate((c0_ref, c1_ref, c2_ref)):
        hs = slice(gi * hpg, (gi + 1) * hpg)
        q, k_new = qn_sc[hs, :], kn_sc[hs, :]
        v_new = zh_ref[0, 2 * nh + gi * hpg:2 * nh + (gi + 1) * hpg, :]
        kvn_ref[0, gi, 0] = k_new
        kvn_ref[0, gi, 1] = v_new
        kc = c_ref[0, 0, :, 0:hpg, :]
        vc = c_ref[0, 0, :, hpg:2 * hpg, :]
        s = jnp.sum(kc * q[None], axis=-1, keepdims=True) + bias_ref[gi]
        s_new = jnp.sum(q * k_new, axis=-1, keepdims=True) + bias0_ref[gi]
        m = jnp.maximum(jnp.max(s, axis=0), s_new)
        e = jnp.exp(s - m[None])
        e_new = jnp.exp(s_new - m)
        den = jnp.sum(e, axis=0) + e_new
        outs.append((jnp.sum(e * vc, axis=0) + e_new * v_new) / den)
        lses.append(m + jnp.log(den))
    top = jnp.maximum(jnp.maximum(lses[0], lses[1]), lses[2])
    wts = [jnp.exp(l - top) for l in lses]
    attn_ref[0] = (wts[0] * outs[0] + wts[1] * outs[1] + wts[2] * outs[2]) / (wts[0] + wts[1] + wts[2])

    c = GM_WIDTH
    u = _gelu(zr_ref[0, :, 0:c])
    vs = _rms(_gelu(zr_ref[0, :, c:2 * c])) * gsgu_ref[...]
    sguv_ref[0] = vs
    sguo_ref[0] = u * (w00_ref[...] * vs + b0_ref[...])
    ga_ref[0] = jax.nn.sigmoid(zr_ref[0, :, 2 * c:2 * c + D_MODEL])
    gb_ref[0] = jax.nn.sigmoid(zr_ref[0, :, 2 * c + D_MODEL:2 * c + 2 * D_MODEL])


def _sample_mix(z, caches, layer, g_q, g_k, g_sgu, w_sgu, b_sgu, bias_s, bias0_s):
    nb = z.shape[0]
    hpg = HEADS_PER_GROUP
    zh = z[:, :3 * ATTN_WIDTH].reshape(nb, 3 * N_ATTN_HEADS, HEAD_DIM)
    rest = 2 * GM_WIDTH + 2 * D_MODEL
    zr = z[:, 3 * ATTN_WIDTH:].reshape(nb, 1, rest)
    w00 = jnp.repeat(w_sgu[:, 0, 0], BLK).reshape(1, GM_WIDTH)
    b0 = jnp.repeat(b_sgu[:, 0], BLK).reshape(1, GM_WIDTH)
    cache_views, cache_specs = [], []
    for c, (_, dil) in zip(caches, DILATION_GROUPS):
        depth, _, length = c.shape[:3]
        cache_views.append(c.reshape(depth, nb, length // dil, dil * KV_ROWS, HEAD_DIM))
        cache_specs.append(pl.BlockSpec((1, 1, BLK, KV_ROWS, HEAD_DIM), lambda b: (layer, b, 0, 0, 0)))
    vec = lambda width: pl.BlockSpec((1, width), lambda b: (0, 0))
    per_b = lambda width: pl.BlockSpec((1, 1, width), lambda b: (b, 0, 0))
    win = (3 * _nbytes((BLK, KV_ROWS, HEAD_DIM), F32) + _nbytes((1, 4 * rest), F32)
           + _nbytes((N_GROUPS, BLK + 1, hpg, HEAD_DIM), F32))
    return pl.pallas_call(
        _sample_mix_kernel,
        grid=(nb,),
        in_specs=[pl.BlockSpec((1, 3 * N_ATTN_HEADS, HEAD_DIM), lambda b: (b, 0, 0)), per_b(rest),
                  *cache_specs, vec(HEAD_DIM), vec(HEAD_DIM), vec(GM_WIDTH), vec(GM_WIDTH),
                  vec(GM_WIDTH),
                  pl.BlockSpec((N_GROUPS, BLK, hpg, HEAD_DIM), lambda b: (0, 0, 0, 0)),
                  pl.BlockSpec((N_GROUPS, hpg, HEAD_DIM), lambda b: (0, 0, 0))],
        out_specs=[pl.BlockSpec((1, N_GROUPS, 2, hpg, HEAD_DIM), lambda b: (b, 0, 0, 0, 0)),
                   pl.BlockSpec((1, hpg, HEAD_DIM), lambda b: (b, 0, 0)),
                   per_b(GM_WIDTH), per_b(GM_WIDTH), per_b(D_MODEL), per_b(D_MODEL)],
        out_shape=[jax.ShapeDtypeStruct((nb, N_GROUPS, 2, hpg, HEAD_DIM), F32),
                   jax.ShapeDtypeStruct((nb, hpg, HEAD_DIM), F32),
                   jax.ShapeDtypeStruct((nb, 1, GM_WIDTH), F32),
                   jax.ShapeDtypeStruct((nb, 1, GM_WIDTH), F32),
                   jax.ShapeDtypeStruct((nb, 1, D_MODEL), F32),
                   jax.ShapeDtypeStruct((nb, 1, D_MODEL), F32)],
        scratch_shapes=[pltpu.VMEM((N_ATTN_HEADS, HEAD_DIM), F32)] * 2,
        compiler_params=_params(("parallel",), win),
        name="decode_mixers",
    )(zh, zr, *cache_views, g_q.reshape(1, HEAD_DIM), g_k.reshape(1, HEAD_DIM),
      g_sgu.reshape(1, GM_WIDTH), w00, b0, bias_s, bias0_s)


ROLL_TOKENS = 512


def _roll_kernel(chunk_ref, next_ref, new_ref, o_ref):
    kept = chunk_ref.shape[0] - KV_ROWS
    o_ref[0:kept, :] = chunk_ref[KV_ROWS:, :]
    last = pl.program_id(2) == pl.num_programs(2) - 1

    @pl.when(last)
    def _():
        o_ref[kept:, :] = new_ref[...]

    @pl.when(jnp.logical_not(last))
    def _():
        o_ref[kept:, :] = next_ref[...]


def _roll_cache(cache, new_rows):
    depth, nb, length = cache.shape[:3]
    tokens = min(length, ROLL_TOKENS)
    rows = tokens * KV_ROWS
    flat = cache.reshape(depth, nb, length * KV_ROWS, HEAD_DIM)
    token = (None, None, KV_ROWS, HEAD_DIM)
    out = pl.pallas_call(
        _roll_kernel,
        grid=(depth, nb, length // tokens),
        in_specs=[pl.BlockSpec((None, None, rows, HEAD_DIM), lambda d, b, c: (d, b, c, 0)),
                  pl.BlockSpec(token, lambda d, b, c: (d, b, jnp.minimum((c + 1) * tokens, length - 1), 0)),
                  pl.BlockSpec(token, lambda d, b, c: (d, b, 0, 0))],
        out_specs=pl.BlockSpec((None, None, rows, HEAD_DIM), lambda d, b, c: (d, b, c, 0)),
        out_shape=jax.ShapeDtypeStruct(flat.shape, cache.dtype),
        compiler_params=_params(("parallel", "parallel", "arbitrary"), 2 * _nbytes((rows, HEAD_DIM), F32)),
        name="roll_cache",
    )(flat, flat, new_rows.reshape(depth, nb, KV_ROWS, HEAD_DIM))
    return out.reshape(cache.shape)


def _t5_bucket(dist):
    dist = np.asarray(dist)
    max_exact = NUM_BUCKETS // 2
    large = max_exact + (np.log(np.maximum(dist, 1) / max_exact) / np.log(REL_MAX_DIST / max_exact)
                         * (NUM_BUCKETS - max_exact)).astype(np.int32)
    large = np.minimum(large, NUM_BUCKETS - 1)
    return np.where(dist < max_exact, dist, large).astype(np.int32)


def _bias_tables(rel_bias):
    hpg = HEADS_PER_GROUP
    prompt, dec, dec0 = [], [], []
    for gi, (_, dil) in enumerate(DILATION_GROUPS):
        heads = rel_bias[:, gi * hpg:(gi + 1) * hpg].astype(F32)
        by_off = jnp.take(heads, _t5_bucket(np.arange(BLK + 1) * dil), axis=0).T
        w = jnp.concatenate([by_off[:, ::-1], jnp.full((hpg, BLK), NEG_INF, F32)], axis=1)
        tab = jnp.tile(w, (1, BLK))[:, :BLK * 2 * BLK].reshape(hpg, BLK, 2 * BLK)
        prompt.append(tab)
        dec.append(jnp.broadcast_to(by_off[:, :0:-1].T[:, :, None], (BLK, hpg, HEAD_DIM)))
        dec0.append(jnp.broadcast_to(by_off[:, 0:1], (hpg, HEAD_DIM)))
    return jnp.stack(prompt), jnp.stack(dec), jnp.stack(dec0)


def _prepare_weights(w_in, w_attn_br, w_sgu_br, w_out, w_ffn_gate, w_ffn_up, w_ffn_down,
                     w_ple_gate, w_ple_proj):
    return dict(
        w_in=w_in, w_gate=w_ffn_gate, w_up=w_ffn_up,
        w_kv=w_in[:, :, ATTN_WIDTH:3 * ATTN_WIDTH].astype(BF16),
        w_attn_br=w_attn_br.astype(BF16), w_sgu_br=w_sgu_br.astype(BF16), w_out=w_out.astype(BF16),
        w_down=w_ffn_down.astype(BF16),
        w_ple_gate=w_ple_gate.astype(BF16), w_ple_proj=w_ple_proj.astype(BF16))


DOWN_TK = 2816


def _layer(prompt, sample, pp, ps, caches, wts, layer, depth, kv_prev, tables, g_mix_next, g_q, g_k,
           g_sgu, w_sgu, b_sgu, g_ffn, g_ple, batch, seq):
    (xp, h, inv), (xs, hs, invs) = prompt, sample
    tm, nb = 1024, xs.shape[0]
    bias_tab, bias_s, bias0_s = tables
    w_in = wts["w_in"]

    q_hm, zq = _q_proj(h, inv, hs, invs, w_in, layer, g_q, batch, seq, RES_TM)
    uv, zu = _proj_act(h, inv, hs, invs, w_in, layer, COL_U * TN // PTN, 2 * GM_WIDTH // PTN, "gelu",
                       RES_TM)
    gates, zg = _proj_act(h, inv, hs, invs, w_in, layer, COL_GATE * TN // PTN, 2 * D_MODEL // PTN,
                          "sigmoid", RES_TM)
    ks, vs, kvs = [], [], []
    for gi in range(N_GROUPS):
        k_hm, v_hm, kv = _kv_proj(h, inv, wts["w_kv"], layer, depth, g_k, gi, kv_prev[gi], batch, seq, 512)
        ks.append(k_hm)
        vs.append(v_hm)
        kvs.append(kv)
    zkv = _proj_rows(hs, invs, wts["w_kv"], layer, 2 * N_GROUPS)

    attn_p = _attention(q_hm, ks, vs, bias_tab, batch, seq).reshape(batch * seq, ATTN_OUT)
    sgu_p = _sgu(uv, g_sgu, w_sgu, b_sgu, 512)
    z = jnp.concatenate([zq, zkv, zu, zg], axis=1)
    kvn, attn_s, sguv, sgu_s, ga_s, gb_s = _sample_mix(z, caches, layer, g_q, g_k, g_sgu, w_sgu, b_sgu,
                                                       bias_s, bias0_s)

    wa, wb = wts["w_attn_br"], wts["w_sgu_br"]
    merged_p = _merge(attn_p, sgu_p, wa, wb, layer, gates, gates, D_MODEL // TN, BF16, tm)
    merged_s = _merge(attn_s.reshape(nb, ATTN_OUT), sgu_s.reshape(nb, GM_WIDTH), wa, wb, layer,
                      ga_s.reshape(nb, D_MODEL), gb_s.reshape(nb, D_MODEL), 0, F32, nb)
    xp, h, inv = _mm_res(merged_p, wts["w_out"], layer, xp, RES_TM, D_MODEL, g_ffn, BF16)
    xs, hs, invs = _mm_res(merged_s, wts["w_out"], layer, xs, nb, D_MODEL, g_ffn, F32)

    ff_p, ff_s = _glu(h, inv, hs, invs, wts["w_gate"], wts["w_up"], layer, tm)
    xp, h, inv = _mm_res(ff_p, wts["w_down"], layer, xp, tm, DOWN_TK, g_ple, BF16)
    xs, hs, invs = _mm_res(ff_s, wts["w_down"], layer, xs, nb, DOWN_TK, g_ple, F32)

    wg, wp = wts["w_ple_gate"], wts["w_ple_proj"]
    if g_mix_next is None:
        prompt = (_ple(h, inv, pp, wg, wp, layer, xp, tm), None, None)
        sample = (_ple(hs, invs, ps, wg, wp, layer, xs, nb), None, None)
    else:
        prompt = _ple(h, inv, pp, wg, wp, layer, xp, tm, g_mix_next, BF16)
        sample = _ple(hs, invs, ps, wg, wp, layer, xs, nb, g_mix_next, F32)
    return prompt, sample, kvs, kvn, sguv


def kernel(x_prompt, x_sample, cache_kv_w128, cache_kv_w512, cache_kv_w2048, p_prompt, p_sample,
           rel_bias, g_mix, w_in, g_q, g_k, g_sgu, w_sgu, b_sgu, w_attn_br, w_sgu_br, w_out, g_ffn,
           w_ffn_gate, w_ffn_up, w_ffn_down, g_ple, w_ple_gate, w_ple_proj):
    batch, seq, _ = x_prompt.shape
    nb = x_sample.shape[0]
    depth = w_in.shape[0]
    caches = (cache_kv_w128, cache_kv_w512, cache_kv_w2048)
    tables = _bias_tables(rel_bias)
    wts = _prepare_weights(w_in, w_attn_br, w_sgu_br, w_out, w_ffn_gate, w_ffn_up, w_ffn_down,
                           w_ple_gate, w_ple_proj)

    xp = x_prompt.reshape(batch * seq, D_MODEL)
    xs = x_sample.reshape(nb, D_MODEL)
    pp = p_prompt.reshape(depth, batch * seq, PLE_DIM)
    ps = p_sample.reshape(depth, nb, PLE_DIM)
    prompt = (xp, _rmsnorm(xp, g_mix[0], BF16, 256), jnp.ones((batch * seq, 1), F32))
    sample = (xs, _rmsnorm(xs, g_mix[0], F32, nb), jnp.ones((nb, 1), F32))
    kv_p = [None] * N_GROUPS
    kv_new = [[] for _ in range(N_GROUPS)]
    sgu_v = []
    for i in range(depth):
        g_mix_next = g_mix[i + 1] if i + 1 < depth else None
        params = (g_mix_next, g_q[i], g_k[i], g_sgu[i], w_sgu[i], b_sgu[i], g_ffn[i], g_ple[i])
        prompt, sample, kv_p, kvn, sguv = _layer(prompt, sample, pp, ps, caches, wts, i, depth, kv_p,
                                                 tables, *params, batch, seq)
        for gi in range(N_GROUPS):
            kv_new[gi].append(kvn[:, gi])
        sgu_v.append(sguv)

    xp, xs = prompt[0], sample[0]
    kv_p = [kv.reshape(depth, batch, -1, 2, HEADS_PER_GROUP, HEAD_DIM) for kv in kv_p]
    kv_s = [_roll_cache(c, jnp.stack(kv_new[gi], axis=0)) for gi, c in enumerate(caches)]
    return (xp.reshape(batch, seq, D_MODEL), xs.reshape(nb, 1, D_MODEL), *kv_p, *kv_s,
            jnp.stack(sgu_v, axis=0))
```

```python
import functools

import numpy as np
import jax
import jax.numpy as jnp
from jax import lax
from jax.experimental import pallas as pl
from jax.experimental.pallas import tpu as pltpu

D_MODEL = 4096
HEAD_DIM = 128
DILATION_GROUPS = ((128, 1), (512, 4), (2048, 16))
N_GROUPS = 3
HEADS_PER_GROUP = 8
N_ATTN_HEADS = 24
ATTN_WIDTH = 3072
ATTN_OUT = 1024
BLK = 128
KV_ROWS = 2 * HEADS_PER_GROUP
GM_WIDTH = 2048
GM_GROUPS = 16
PLE_DIM = 256
NUM_BUCKETS = 32
REL_MAX_DIST = 2048
NORM_EPS = 1e-6
NEG_INF = -1e30
Q_SCALE = HEAD_DIM ** -0.5

TN = 1024
COL_U, COL_GATE = 9, 13

V7X_VMEM_REQUEST_CAP = 60000 * 1024
COMPILER_TEMP_BYTES = 16 * 1024 * 1024

F32 = jnp.float32
BF16 = jnp.bfloat16


def _params(sems, window_bytes, scratch_bytes=0):
    limit = min(V7X_VMEM_REQUEST_CAP, 2 * window_bytes + scratch_bytes + COMPILER_TEMP_BYTES)
    return pltpu.CompilerParams(dimension_semantics=sems, vmem_limit_bytes=int(limit))


def _nbytes(shape, dtype):
    return int(np.prod(shape)) * jnp.dtype(dtype).itemsize


SQRT_2_OVER_PI = 0.7978845608028654
GELU_CUBIC = 0.044715


def _gelu(x):
    return x * (0.5 * (1.0 + jnp.tanh(SQRT_2_OVER_PI * (x + GELU_CUBIC * (x * x * x)))))


def _silu(x):
    return x * jax.nn.sigmoid(x)


_ACT = {"gelu": _gelu, "sigmoid": jax.nn.sigmoid}


def _rms(x):
    return x * lax.rsqrt(jnp.mean(x * x, axis=-1, keepdims=True) + NORM_EPS)


def _dot(a, b):
    return jnp.dot(a.astype(BF16), b.astype(BF16), preferred_element_type=F32)


def _dot_nt(a, b):
    return lax.dot_general(a, b, (((1,), (1,)), ((), ())), preferred_element_type=F32)


def _wspec(layer, k, tn, col_map):
    return pl.BlockSpec((None, k, tn), lambda *ids: (layer, 0, col_map(*ids)))


def _rmsnorm_kernel(x_ref, g_ref, o_ref):
    o_ref[...] = (_rms(x_ref[...]) * g_ref[...]).astype(o_ref.dtype)


def _rmsnorm(x, g, out_dtype, tm):
    m, d = x.shape
    return pl.pallas_call(
        _rmsnorm_kernel,
        grid=(m // tm,),
        in_specs=[pl.BlockSpec((tm, d), lambda i: (i, 0)), pl.BlockSpec((1, d), lambda i: (0, 0))],
        out_specs=pl.BlockSpec((tm, d), lambda i: (i, 0)),
        out_shape=jax.ShapeDtypeStruct((m, d), out_dtype),
        compiler_params=_params(("parallel",), _nbytes((tm, d), F32) + _nbytes((tm, d), out_dtype)),
        name="rmsnorm",
    )(x, g.reshape(1, d))


def _proj_rows_kernel(x_ref, inv_ref, w_ref, o_ref):
    o_ref[...] = _dot(x_ref[...], w_ref[...]) * inv_ref[...]


def _proj_rows(h, inv, w, layer, n_tiles):
    m, k = h.shape
    win = _nbytes((m, k), h.dtype) + _nbytes((k, TN), BF16) + _nbytes((m, TN), F32)
    return pl.pallas_call(
        _proj_rows_kernel,
        grid=(n_tiles,),
        in_specs=[pl.BlockSpec((m, k), lambda j: (0, 0)), pl.BlockSpec((m, 1), lambda j: (0, 0)),
                  _wspec(layer, k, TN, lambda j: j)],
        out_specs=pl.BlockSpec((m, TN), lambda j: (0, j)),
        out_shape=jax.ShapeDtypeStruct((m, n_tiles * TN), F32),
        compiler_params=_params(("parallel",), win),
        name="proj_rows",
    )(h, inv, w)


PTN = 1024
RES_TM = 512


def _resident_cast(w_refs, wb_refs):
    for w_ref, wb_ref in zip(w_refs, wb_refs):
        wb_ref[...] = w_ref[...].astype(BF16)


def _row_specs(tm, k, nb):
    return [pl.BlockSpec((tm, k), lambda j, i: (i, 0)), pl.BlockSpec((tm, 1), lambda j, i: (i, 0)),
            pl.BlockSpec((nb, k), lambda j, i: (0, 0)), pl.BlockSpec((nb, 1), lambda j, i: (0, 0))]


def _proj_act_kernel(x_ref, inv_ref, xs_ref, invs_ref, w_ref, o_ref, os_ref, wb, *, act):
    @pl.when(pl.program_id(1) == 0)
    def _():
        _resident_cast([w_ref], [wb])
        os_ref[...] = _dot(xs_ref[...], wb[...]) * invs_ref[...]

    o_ref[...] = _ACT[act](_dot(x_ref[...], wb[...]) * inv_ref[...]).astype(o_ref.dtype)


def _proj_act(h, inv, hs, invs, w, layer, col0, n_tiles, act, tm):
    (m, k), nb = h.shape, hs.shape[0]
    win = _nbytes((tm, k), BF16) + _nbytes((k, PTN), F32) + _nbytes((tm, PTN), BF16)
    return pl.pallas_call(
        functools.partial(_proj_act_kernel, act=act),
        grid=(n_tiles, m // tm),
        in_specs=[*_row_specs(tm, k, nb), _wspec(layer, k, PTN, lambda j, i: j + col0)],
        out_specs=[pl.BlockSpec((tm, PTN), lambda j, i: (i, j)),
                   pl.BlockSpec((nb, PTN), lambda j, i: (0, j))],
        out_shape=[jax.ShapeDtypeStruct((m, n_tiles * PTN), BF16),
                   jax.ShapeDtypeStruct((nb, n_tiles * PTN), F32)],
        scratch_shapes=[pltpu.VMEM((k, PTN), BF16)],
        compiler_params=_params(("arbitrary", "arbitrary"), win, _nbytes((k, PTN), BF16)),
        name="proj_" + act,
    )(h, inv, hs, invs, w)


def _q_kernel(x_ref, inv_ref, xs_ref, invs_ref, w_ref, g_ref, o_ref, os_ref, wb):
    @pl.when(pl.program_id(1) == 0)
    def _():
        _resident_cast([w_ref], [wb])
        os_ref[...] = _dot(xs_ref[...], wb[...]) * invs_ref[...]

    acc = _dot(x_ref[...], wb[...]) * inv_ref[...]
    for hh in range(PTN // HEAD_DIM):
        a = acc[:, hh * HEAD_DIM:(hh + 1) * HEAD_DIM]
        o_ref[0, hh] = ((_rms(a) * g_ref[...]) * Q_SCALE).astype(o_ref.dtype)


def _q_proj(h, inv, hs, invs, w, layer, g_q, batch, seq, tm):
    (m, k), nb = h.shape, hs.shape[0]
    tpb = seq // tm
    heads = PTN // HEAD_DIM
    win = _nbytes((tm, k), BF16) + _nbytes((k, PTN), F32) + _nbytes((tm, PTN), BF16)
    return pl.pallas_call(
        _q_kernel,
        grid=(ATTN_WIDTH // PTN, m // tm),
        in_specs=[*_row_specs(tm, k, nb), _wspec(layer, k, PTN, lambda j, i: j),
                  pl.BlockSpec((1, HEAD_DIM), lambda j, i: (0, 0))],
        out_specs=[pl.BlockSpec((1, heads, tm, HEAD_DIM), lambda j, i: (i // tpb, j, i % tpb, 0)),
                   pl.BlockSpec((nb, PTN), lambda j, i: (0, j))],
        out_shape=[jax.ShapeDtypeStruct((batch, N_ATTN_HEADS, seq, HEAD_DIM), BF16),
                   jax.ShapeDtypeStruct((nb, ATTN_WIDTH), F32)],
        scratch_shapes=[pltpu.VMEM((k, PTN), BF16)],
        compiler_params=_params(("arbitrary", "arbitrary"), win, _nbytes((k, PTN), BF16)),
        name="q_proj",
    )(h, inv, hs, invs, w, g_q.reshape(1, HEAD_DIM))


def _kv_kernel(x_ref, inv_ref, xs_ref, invs_ref, w_ref, g_ref, kv_prev_ref, hm_ref, kv_ref, os_ref, wb,
               *, rows):
    del kv_prev_ref

    @pl.when(pl.program_id(1) == 0)
    def _():
        _resident_cast([w_ref], [wb])
        os_ref[...] = _dot(xs_ref[...], wb[...]) * invs_ref[...]

    acc = _dot(x_ref[...], wb[...]) * inv_ref[...]
    tm = acc.shape[0]
    is_key = pl.program_id(0) == 0
    kv_rows = kv_ref.reshape(rows * HEADS_PER_GROUP, HEAD_DIM)
    for hh in range(HEADS_PER_GROUP):
        a = acc[:, hh * HEAD_DIM:(hh + 1) * HEAD_DIM]
        y = jnp.where(is_key, _rms(a) * g_ref[...], a)
        hm_ref[0, hh] = y.astype(BF16)
        kv_rows[pl.ds(hh, rows, stride=HEADS_PER_GROUP), :] = y[tm - rows:]


def _kv_proj(h, inv, hs, invs, w, layer, depth, g_k, gi, kv_prev, batch, seq, tm):
    (m, k), nb = h.shape, hs.shape[0]
    tpb = seq // tm
    keep = min(DILATION_GROUPS[gi][0], seq)
    rows = min(keep, tm)
    kv_tiles = keep // rows
    hpg = HEADS_PER_GROUP
    kv_map = lambda j, i: (layer, i // tpb, jnp.maximum(i % tpb - (tpb - kv_tiles), 0), j, 0)
    kv_shape = jax.ShapeDtypeStruct((depth, batch, keep, KV_ROWS, HEAD_DIM), F32)
    win = _nbytes((tm, k), BF16) + _nbytes((tm, TN), BF16) + _nbytes((rows, hpg, HEAD_DIM), F32)
    resident = _nbytes((k, TN), F32) + _nbytes((k, TN), BF16)
    if kv_prev is None:
        kv_prev = jnp.zeros(kv_shape.shape, F32)
    return pl.pallas_call(
        functools.partial(_kv_kernel, rows=rows),
        grid=(2, m // tm),
        in_specs=[*_row_specs(tm, k, nb),
                  pl.BlockSpec((None, k, TN), lambda j, i: (layer, 0, ATTN_WIDTH // TN * (1 + j) + gi),
                               pipeline_mode=pl.Buffered(1)),
                  pl.BlockSpec((1, HEAD_DIM), lambda j, i: (0, 0)),
                  pl.BlockSpec(memory_space=pl.ANY)],
        out_specs=[pl.BlockSpec((None, 1, hpg, tm, HEAD_DIM), lambda j, i: (j, i // tpb, 0, i % tpb, 0)),
                   pl.BlockSpec((None, None, rows, hpg, HEAD_DIM), kv_map),
                   pl.BlockSpec((nb, TN), lambda j, i: (0, j))],
        out_shape=[jax.ShapeDtypeStruct((2, batch, hpg, seq, HEAD_DIM), BF16), kv_shape,
                   jax.ShapeDtypeStruct((nb, 2 * TN), F32)],
        scratch_shapes=[pltpu.VMEM((k, TN), BF16)],
        input_output_aliases={6: 1},
        compiler_params=_params(("arbitrary", "arbitrary"), win, resident),
        name="kv_proj",
    )(h, inv, hs, invs, w, g_k.reshape(1, HEAD_DIM), kv_prev)


ATTN_BATCH = 8


def _attn_blocks(qkv, bias):
    scores = [_dot_nt(q, k) + bias for q, k, _ in qkv]
    tops = [jnp.max(s, axis=-1, keepdims=True) for s in scores]
    probs = [jnp.exp(s - m).astype(BF16) for s, m in zip(scores, tops)]
    nds = [jnp.dot(e, jnp.concatenate([v, jnp.ones_like(v)], axis=1), preferred_element_type=F32)
           for e, (_, _, v) in zip(probs, qkv)]
    return [(nd[:, :HEAD_DIM] / nd[:, HEAD_DIM:], m + jnp.log(nd[:, HEAD_DIM:])) for nd, m in zip(nds, tops)]


def _attn_kernel(q0_ref, q1_ref, q2_ref, k0_ref, k1_ref, k2_ref, v0_ref, v1_ref, v2_ref, bias_ref, o_ref,
                 q1f, q2f, k1f, k2f, v1f, v2f, o0_sc, o1_sc, o2_sc, l0_sc, l1_sc, l2_sc, *, seq):
    for src, dst in ((q1_ref, q1f), (q2_ref, q2f), (k1_ref, k1f), (k2_ref, k2f), (v1_ref, v1f), (v2_ref, v2f)):
        dst[...] = src[0, 0].astype(F32)

    def run_group(gi, load, o_sc, l_sc, unroll):
        dil = DILATION_GROUPS[gi][1]
        nblk = seq // dil // BLK

        def rows(r, s0, n):
            if dil == 1:
                return pl.ds(s0 if isinstance(s0, int) else pl.multiple_of(s0, BLK), n)
            return pl.ds(s0 * dil + r, n, stride=dil)

        def blocks(todo, first):
            back = 0 if first else BLK
            bias = bias_ref[gi, 0, :, BLK:2 * BLK] if first else bias_ref[gi, 0]
            qkv = [load(rows(r, s0, BLK), rows(r, s0 - back, BLK + back)) for r, s0 in todo]
            for (r, s0), (o, lse) in zip(todo, _attn_blocks(qkv, bias)):
                o_sc[rows(r, s0, BLK), :] = o
                l_sc[rows(r, s0, BLK), :] = lse

        for r0 in range(0, dil, ATTN_BATCH):
            blocks([(r, 0) for r in range(r0, min(dil, r0 + ATTN_BATCH))], True)

        def body(it, carry):
            blocks([(r, (1 + it * unroll + u) * BLK) for u in range(unroll) for r in range(dil)], False)
            return carry

        if nblk > 1:
            lax.fori_loop(0, (nblk - 1) // unroll, body, 0)

    def bf16_load(q_ref, k_ref, v_ref):
        return lambda qr, kr: (q_ref[0, 0, qr, :], k_ref[0, 0, kr, :], v_ref[0, 0, kr, :])

    def f32_load(qf, kf, vf):
        return lambda qr, kr: (qf[qr, :].astype(BF16), kf[kr, :].astype(BF16), vf[kr, :].astype(BF16))

    run_group(0, bf16_load(q0_ref, k0_ref, v0_ref), o0_sc, l0_sc, 5)
    run_group(1, f32_load(q1f, k1f, v1f), o1_sc, l1_sc, 1)
    run_group(2, f32_load(q2f, k2f, v2f), o2_sc, l2_sc, 1)

    def combine(c, carry):
        rows = pl.ds(pl.multiple_of(c * BLK, BLK), BLK)
        l0, l1, l2 = l0_sc[rows, :], l1_sc[rows, :], l2_sc[rows, :]
        top = jnp.maximum(jnp.maximum(l0, l1), l2)
        w0, w1, w2 = jnp.exp(l0 - top), jnp.exp(l1 - top), jnp.exp(l2 - top)
        mixed = (w0 * o0_sc[rows, :] + w1 * o1_sc[rows, :] + w2 * o2_sc[rows, :]) / (w0 + w1 + w2)
        o_ref[0, rows, :] = mixed.astype(o_ref.dtype)
        return carry

    lax.fori_loop(0, seq // BLK, combine, 0)


def _attention(q_hm, kv_hm, bias_tab, batch, seq):
    slab = (1, 1, seq, HEAD_DIM)
    specs = [pl.BlockSpec(slab, lambda b, h, gi=gi: (b, gi * HEADS_PER_GROUP + h, 0, 0))
             for gi in range(N_GROUPS)]
    for which in range(2):
        specs += [pl.BlockSpec((None,) + slab, lambda b, h, which=which: (which, b, h, 0, 0))] * N_GROUPS
    specs.append(pl.BlockSpec((N_GROUPS, 1, BLK, 2 * BLK), lambda b, h: (0, h, 0, 0)))
    win = (10 * _nbytes((seq, HEAD_DIM), BF16) + _nbytes((N_GROUPS, BLK, 2 * BLK), F32)
           + 6 * _nbytes((seq, HEAD_DIM), F32))
    return pl.pallas_call(
        functools.partial(_attn_kernel, seq=seq),
        grid=(batch, HEADS_PER_GROUP),
        in_specs=specs,
        out_specs=pl.BlockSpec((1, seq, HEAD_DIM), lambda b, h: (b, 0, h)),
        out_shape=jax.ShapeDtypeStruct((batch, seq, ATTN_OUT), BF16),
        scratch_shapes=[pltpu.VMEM((seq, HEAD_DIM), F32)] * 12,
        compiler_params=_params(("parallel", "arbitrary"), win),
        name="dilated_attention",
    )(q_hm, q_hm, q_hm, *kv_hm, *kv_hm, bias_tab)


def _sgu_kernel(u_ref, gv_ref, g_ref, w_ref, bt_ref, o_ref, vs_sc):
    gv = gv_ref[...].astype(F32)
    vs_sc[...] = (_rms(gv) * g_ref[...]).astype(BF16)
    tm = gv.shape[0]
    row = lax.broadcasted_iota(jnp.int32, (BLK, BLK), 0)
    col = lax.broadcasted_iota(jnp.int32, (BLK, BLK), 1)
    tril = (row >= col).astype(F32)
    for g in range(GM_GROUPS):
        cols = slice(g * BLK, (g + 1) * BLK)
        wg = (w_ref[g] * tril).astype(BF16)
        bg = bt_ref[:, g:g + 1]
        for c in range(tm // BLK):
            rows = slice(c * BLK, (c + 1) * BLK)
            mix = _dot(wg, vs_sc[rows, cols]) + bg
            o_ref[rows, cols] = (u_ref[rows, cols].astype(F32) * mix).astype(o_ref.dtype)


def _sgu(uv, g_sgu, w_sgu, b_sgu, tm):
    m, c = uv.shape[0], GM_WIDTH
    win = 3 * _nbytes((tm, c), BF16) + _nbytes((GM_GROUPS, BLK, BLK), F32)
    return pl.pallas_call(
        _sgu_kernel,
        grid=(m // tm,),
        in_specs=[pl.BlockSpec((tm, c), lambda i: (i, 0)),
                  pl.BlockSpec((tm, c), lambda i: (i, 1)),
                  pl.BlockSpec((1, c), lambda i: (0, 0)),
                  pl.BlockSpec((GM_GROUPS, BLK, BLK), lambda i: (0, 0, 0)),
                  pl.BlockSpec((BLK, GM_GROUPS), lambda i: (0, 0))],
        out_specs=pl.BlockSpec((tm, c), lambda i: (i, 0)),
        out_shape=jax.ShapeDtypeStruct((m, c), BF16),
        scratch_shapes=[pltpu.VMEM((tm, c), BF16)],
        compiler_params=_params(("parallel",), win),
        name="spatial_gating",
    )(uv, uv, g_sgu.reshape(1, c), w_sgu, b_sgu.T)


def _merge_kernel(a_ref, s_ref, wa_ref, wb_ref, ga_ref, gb_ref, o_ref):
    a = _dot(a_ref[...], wa_ref[...])
    b = _dot(s_ref[...], wb_ref[...])
    o_ref[...] = (ga_ref[...].astype(F32) * a + gb_ref[...].astype(F32) * b).astype(o_ref.dtype)


def _merge(attn, sgu, wa, wb, layer, ga, gb, gb_col0, out_dtype, tm):
    m = attn.shape[0]
    win = (_nbytes((tm, ATTN_OUT), attn.dtype) + _nbytes((tm, GM_WIDTH), sgu.dtype)
           + _nbytes((ATTN_OUT + GM_WIDTH, TN), BF16) + 2 * _nbytes((tm, TN), ga.dtype)
           + _nbytes((tm, TN), out_dtype))
    row = lambda width: pl.BlockSpec((tm, width), lambda i, j: (i, 0))
    tile = pl.BlockSpec((tm, TN), lambda i, j: (i, j))
    return pl.pallas_call(
        _merge_kernel,
        grid=(m // tm, D_MODEL // TN),
        in_specs=[row(ATTN_OUT), row(GM_WIDTH),
                  _wspec(layer, ATTN_OUT, TN, lambda i, j: j),
                  _wspec(layer, GM_WIDTH, TN, lambda i, j: j), tile,
                  pl.BlockSpec((tm, TN), lambda i, j: (i, j + gb_col0))],
        out_specs=tile,
        out_shape=jax.ShapeDtypeStruct((m, D_MODEL), out_dtype),
        compiler_params=_params(("parallel", "arbitrary"), win),
        name="branch_merge",
    )(attn, sgu, wa, wb, ga, gb)


def _emit_scaled(y, g_ref, xg_ref, inv_ref, ss_sc, j, n_col_tiles):
    xg_ref[...] = (y * g_ref[...]).astype(xg_ref.dtype)
    ss = jnp.sum(y * y, axis=-1, keepdims=True)

    @pl.when(j == 0)
    def _():
        ss_sc[...] = ss

    @pl.when(j > 0)
    def _():
        ss_sc[...] += ss

    @pl.when(j == n_col_tiles - 1)
    def _():
        inv_ref[...] = lax.rsqrt(ss_sc[...] * (1.0 / (n_col_tiles * y.shape[1])) + NORM_EPS)


def _norm_out(m, n, tm, tn, xg_dtype, col_of):
    in_spec = pl.BlockSpec((1, tn), lambda *ids: (0, col_of(*ids)))
    out_specs = [pl.BlockSpec((tm, tn), lambda *ids: (ids[0], col_of(*ids))),
                 pl.BlockSpec((tm, 1), lambda *ids: (ids[0], 0))]
    out_shape = [jax.ShapeDtypeStruct((m, n), xg_dtype), jax.ShapeDtypeStruct((m, 1), F32)]
    return in_spec, out_specs, out_shape, pltpu.VMEM((tm, 1), F32)


def _mm_res_kernel(x_ref, w_ref, r_ref, *rest, k_rows, norm, grid):
    o_ref = rest[1] if norm else rest[0]
    x, w = x_ref[...], w_ref[...]
    tk = w.shape[0]
    if k_rows % tk:
        k0 = pl.program_id(2) * tk
        w = jnp.where(k0 + lax.broadcasted_iota(jnp.int32, w.shape, 0) < k_rows, w, jnp.zeros_like(w))
        x = jnp.where(k0 + lax.broadcasted_iota(jnp.int32, x.shape, 1) < k_rows, x, jnp.zeros_like(x))
    part = _dot(x, w)

    @pl.when(pl.program_id(2) == 0)
    def _():
        o_ref[...] = r_ref[...] + part

    @pl.when(pl.program_id(2) > 0)
    def _():
        o_ref[...] += part

    if norm:
        g_ref, _, xg_ref, inv_ref, ss_sc = rest

        @pl.when(pl.program_id(2) == grid[2] - 1)
        def _():
            _emit_scaled(o_ref[...], g_ref, xg_ref, inv_ref, ss_sc, pl.program_id(1), grid[1])


def _mm_res(x, w, layer, res, tm, tn, tk, g_next=None, xg_dtype=None):
    m, k = x.shape
    n = w.shape[2]
    norm = g_next is not None
    win = _nbytes((tm, tk), x.dtype) + _nbytes((tk, tn), BF16) + 2 * _nbytes((tm, tn), F32)
    ins = [x, w, res]
    in_specs = [pl.BlockSpec((tm, tk), lambda i, j, kk: (i, kk)),
                pl.BlockSpec((None, tk, tn), lambda i, j, kk: (layer, kk, j)),
                pl.BlockSpec((tm, tn), lambda i, j, kk: (i, j))]
    out_specs = [pl.BlockSpec((tm, tn), lambda i, j, kk: (i, j))]
    out_shape = [jax.ShapeDtypeStruct((m, n), F32)]
    scratch = []
    if norm:
        g_spec, n_specs, n_shapes, ss = _norm_out(m, n, tm, tn, xg_dtype, lambda i, j, kk: j)
        ins.append(g_next.reshape(1, n))
        in_specs.append(g_spec)
        out_specs += n_specs
        out_shape += n_shapes
        scratch.append(ss)
        win += _nbytes((tm, tn), xg_dtype)
    grid = (m // tm, n // tn, pl.cdiv(k, tk))
    out = pl.pallas_call(
        functools.partial(_mm_res_kernel, k_rows=k, norm=norm, grid=grid),
        grid=grid,
        in_specs=in_specs,
        out_specs=out_specs,
        out_shape=out_shape,
        scratch_shapes=scratch,
        compiler_params=_params(("parallel", "arbitrary", "arbitrary"), win),
        name="matmul_residual",
    )(*ins)
    return out if norm else out[0]


GLU_TN = 256


def _glu_row_copy(x_hbm, xbuf, sems, i, tm):
    slot = i % 2
    return pltpu.make_async_copy(x_hbm.at[pl.ds(i * tm, tm), :], xbuf.at[slot], sems.at[slot])


def _glu_kernel(inv_ref, xs_ref, invs_ref, wg_ref, wu_ref, x_hbm, o_ref, os_ref, wbg, wbu, xbuf, sems, *, tm):
    j, nj = pl.program_id(0), pl.num_programs(0)
    n_row_tiles = o_ref.shape[0] // tm
    copy = functools.partial(_glu_row_copy, x_hbm, xbuf, sems, tm=tm)

    @pl.when(j == 0)
    def _():
        copy(0).start()

    _resident_cast([wg_ref, wu_ref], [wbg, wbu])
    xs, invs = xs_ref[...], invs_ref[...]
    os_ref[...] = _silu(_dot(xs, wbg[...]) * invs) * (_dot(xs, wbu[...]) * invs)

    for i in range(n_row_tiles):
        copy(i).wait()
        if i + 1 < n_row_tiles:
            copy(i + 1).start()
        else:
            @pl.when(j + 1 < nj)
            def _():
                copy(0).start()

        rows = slice(i * tm, (i + 1) * tm)
        x, inv = xbuf[i % 2], inv_ref[rows, :]
        o_ref[rows, :] = (_silu(_dot(x, wbg[...]) * inv) * (_dot(x, wbu[...]) * inv)).astype(o_ref.dtype)


def _glu(h, inv, hs, invs, wg, wu, layer, tm):
    (m, k), nb = h.shape, hs.shape[0]
    n = wg.shape[2]
    assert (m // tm) % 2 == 0, "tile 0 must return to ring slot 0 on the next step"
    win = 2 * _nbytes((k, GLU_TN), F32) + _nbytes((m, GLU_TN), BF16)
    scratch = 2 * _nbytes((k, GLU_TN), BF16) + 2 * _nbytes((tm, k), BF16) + _nbytes((m, HEAD_DIM), F32)
    wspec = _wspec(layer, k, GLU_TN, lambda j: j)
    return pl.pallas_call(
        functools.partial(_glu_kernel, tm=tm),
        grid=(n // GLU_TN,),
        in_specs=[pl.BlockSpec((m, 1), lambda j: (0, 0), pipeline_mode=pl.Buffered(1)),
                  pl.BlockSpec((nb, k), lambda j: (0, 0)), pl.BlockSpec((nb, 1), lambda j: (0, 0)),
                  wspec, wspec, pl.BlockSpec(memory_space=pl.ANY)],
        out_specs=[pl.BlockSpec((m, GLU_TN), lambda j: (0, j)),
                   pl.BlockSpec((nb, GLU_TN), lambda j: (0, j))],
        out_shape=[jax.ShapeDtypeStruct((m, n), BF16), jax.ShapeDtypeStruct((nb, n), F32)],
        scratch_shapes=[pltpu.VMEM((k, GLU_TN), BF16), pltpu.VMEM((k, GLU_TN), BF16),
                        pltpu.VMEM((2, tm, k), BF16), pltpu.SemaphoreType.DMA((2,))],
        compiler_params=_params(("arbitrary",), win, scratch),
        name="swiglu_up",
    )(inv, hs, invs, wg, wu, h)


PLE_TN = 512


def _ple_kernel(h_ref, inv_ref, p_ref, wg_ref, wp_ref, r_ref, *rest, norm, grid):
    o_ref = rest[1] if norm else rest[0]
    gate = jax.nn.sigmoid(_dot(h_ref[...], wg_ref[...]) * inv_ref[...])
    y = r_ref[...] + gate * _dot(p_ref[...], wp_ref[...])
    o_ref[...] = y
    if norm:
        g_ref, _, xg_ref, invn_ref, ss_sc = rest
        _emit_scaled(y, g_ref, xg_ref, invn_ref, ss_sc, pl.program_id(1), grid[1])


def _ple(h, inv, p, wg, wp, layer, res, tm, g_next=None, xg_dtype=None):
    m, k = h.shape
    tn = PLE_TN
    norm = g_next is not None
    win = (_nbytes((tm, k), h.dtype) + _nbytes((tm, PLE_DIM), F32) + _nbytes((k + PLE_DIM, tn), BF16)
           + 2 * _nbytes((tm, tn), F32))
    tile = pl.BlockSpec((tm, tn), lambda i, j: (i, j))
    ins = [h, inv, p, wg, wp, res]
    in_specs = [pl.BlockSpec((tm, k), lambda i, j: (i, 0)),
                pl.BlockSpec((tm, 1), lambda i, j: (i, 0)),
                pl.BlockSpec((None, tm, PLE_DIM), lambda i, j: (layer, i, 0)),
                _wspec(layer, k, tn, lambda i, j: j),
                _wspec(layer, PLE_DIM, tn, lambda i, j: j), tile]
    out_specs = [tile]
    out_shape = [jax.ShapeDtypeStruct((m, D_MODEL), F32)]
    scratch = []
    if norm:
        g_spec, n_specs, n_shapes, ss = _norm_out(m, D_MODEL, tm, tn, xg_dtype, lambda i, j: j)
        ins.append(g_next.reshape(1, D_MODEL))
        in_specs.append(g_spec)
        out_specs += n_specs
        out_shape += n_shapes
        scratch.append(ss)
        win += _nbytes((tm, tn), xg_dtype)
    grid = (m // tm, D_MODEL // tn)
    out = pl.pallas_call(
        functools.partial(_ple_kernel, norm=norm, grid=grid),
        grid=grid,
        in_specs=in_specs,
        out_specs=out_specs,
        out_shape=out_shape,
        scratch_shapes=scratch,
        compiler_params=_params(("parallel", "arbitrary"), win),
        name="ple_gate",
    )(*ins)
    return out if norm else out[0]


def _sample_mix_kernel(zh_ref, zr_ref, c0_ref, c1_ref, c2_ref, gq_ref, gk_ref, gsgu_ref,
                       w00_ref, b0_ref, bias_ref, bias0_ref,
                       kvn_ref, attn_ref, sguv_ref, sguo_ref, ga_ref, gb_ref, qn_sc, kn_sc):
    nh, hpg = N_ATTN_HEADS, HEADS_PER_GROUP
    qn_sc[...] = (_rms(zh_ref[0, 0:nh, :]) * gq_ref[...]) * Q_SCALE
    kn_sc[...] = _rms(zh_ref[0, nh:2 * nh, :]) * gk_ref[...]

    outs, lses = [], []
    for gi, c_ref in enumerate((c0_ref, c1_ref, c2_ref)):
        hs = slice(gi * hpg, (gi + 1) * hpg)
        q, k_new = qn_sc[hs, :], kn_sc[hs, :]
        v_new = zh_ref[0, 2 * nh + gi * hpg:2 * nh + (gi + 1) * hpg, :]
        kvn_ref[0, gi, 0] = k_new
        kvn_ref[0, gi, 1] = v_new
        kc = c_ref[0, 0, :, 0:hpg, :]
        vc = c_ref[0, 0, :, hpg:2 * hpg, :]
        s = jnp.sum(kc * q[None], axis=-1, keepdims=True) + bias_ref[gi]
        s_new = jnp.sum(q * k_new, axis=-1, keepdims=True) + bias0_ref[gi]
        m = jnp.maximum(jnp.max(s, axis=0), s_new)
        e = jnp.exp(s - m[None])
        e_new = jnp.exp(s_new - m)
        den = jnp.sum(e, axis=0) + e_new
        outs.append((jnp.sum(e * vc, axis=0) + e_new * v_new) / den)
        lses.append(m + jnp.log(den))
    top = jnp.maximum(jnp.maximum(lses[0], lses[1]), lses[2])
    wts = [jnp.exp(l - top) for l in lses]
    attn_ref[0] = (wts[0] * outs[0] + wts[1] * outs[1] + wts[2] * outs[2]) / (wts[0] + wts[1] + wts[2])

    c = GM_WIDTH
    u = _gelu(zr_ref[0, :, 0:c])
    vs = _rms(_gelu(zr_ref[0, :, c:2 * c])) * gsgu_ref[...]
    sguv_ref[0] = vs
    sguo_ref[0] = u * (w00_ref[...] * vs + b0_ref[...])
    ga_ref[0] = jax.nn.sigmoid(zr_ref[0, :, 2 * c:2 * c + D_MODEL])
    gb_ref[0] = jax.nn.sigmoid(zr_ref[0, :, 2 * c + D_MODEL:2 * c + 2 * D_MODEL])


def _sample_mix(z, caches, layer, g_q, g_k, g_sgu, w_sgu, b_sgu, bias_s, bias0_s):
    nb = z.shape[0]
    hpg = HEADS_PER_GROUP
    zh = z[:, :3 * ATTN_WIDTH].reshape(nb, 3 * N_ATTN_HEADS, HEAD_DIM)
    rest = 2 * GM_WIDTH + 2 * D_MODEL
    zr = z[:, 3 * ATTN_WIDTH:].reshape(nb, 1, rest)
    w00 = jnp.repeat(w_sgu[:, 0, 0], BLK).reshape(1, GM_WIDTH)
    b0 = jnp.repeat(b_sgu[:, 0], BLK).reshape(1, GM_WIDTH)
    cache_views, cache_specs = [], []
    for c, (_, dil) in zip(caches, DILATION_GROUPS):
        depth, _, length = c.shape[:3]
        cache_views.append(c.reshape(depth, nb, length // dil, dil * KV_ROWS, HEAD_DIM))
        cache_specs.append(pl.BlockSpec((1, 1, BLK, KV_ROWS, HEAD_DIM), lambda b: (layer, b, 0, 0, 0)))
    vec = lambda width: pl.BlockSpec((1, width), lambda b: (0, 0))
    per_b = lambda width: pl.BlockSpec((1, 1, width), lambda b: (b, 0, 0))
    win = (3 * _nbytes((BLK, KV_ROWS, HEAD_DIM), F32) + _nbytes((1, 4 * rest), F32)
           + _nbytes((N_GROUPS, BLK + 1, hpg, HEAD_DIM), F32))
    return pl.pallas_call(
        _sample_mix_kernel,
        grid=(nb,),
        in_specs=[pl.BlockSpec((1, 3 * N_ATTN_HEADS, HEAD_DIM), lambda b: (b, 0, 0)), per_b(rest),
                  *cache_specs, vec(HEAD_DIM), vec(HEAD_DIM), vec(GM_WIDTH), vec(GM_WIDTH),
                  vec(GM_WIDTH),
                  pl.BlockSpec((N_GROUPS, BLK, hpg, HEAD_DIM), lambda b: (0, 0, 0, 0)),
                  pl.BlockSpec((N_GROUPS, hpg, HEAD_DIM), lambda b: (0, 0, 0))],
        out_specs=[pl.BlockSpec((1, N_GROUPS, 2, hpg, HEAD_DIM), lambda b: (b, 0, 0, 0, 0)),
                   pl.BlockSpec((1, hpg, HEAD_DIM), lambda b: (b, 0, 0)),
                   per_b(GM_WIDTH), per_b(GM_WIDTH), per_b(D_MODEL), per_b(D_MODEL)],
        out_shape=[jax.ShapeDtypeStruct((nb, N_GROUPS, 2, hpg, HEAD_DIM), F32),
                   jax.ShapeDtypeStruct((nb, hpg, HEAD_DIM), F32),
                   jax.ShapeDtypeStruct((nb, 1, GM_WIDTH), F32),
                   jax.ShapeDtypeStruct((nb, 1, GM_WIDTH), F32),
                   jax.ShapeDtypeStruct((nb, 1, D_MODEL), F32),
                   jax.ShapeDtypeStruct((nb, 1, D_MODEL), F32)],
        scratch_shapes=[pltpu.VMEM((N_ATTN_HEADS, HEAD_DIM), F32)] * 2,
        compiler_params=_params(("parallel",), win),
        name="decode_mixers",
    )(zh, zr, *cache_views, g_q.reshape(1, HEAD_DIM), g_k.reshape(1, HEAD_DIM),
      g_sgu.reshape(1, GM_WIDTH), w00, b0, bias_s, bias0_s)


ROLL_TOKENS = 512


def _roll_kernel(chunk_ref, next_ref, new_ref, o_ref):
    kept = chunk_ref.shape[0] - KV_ROWS
    o_ref[0:kept, :] = chunk_ref[KV_ROWS:, :]
    last = pl.program_id(2) == pl.num_programs(2) - 1

    @pl.when(last)
    def _():
        o_ref[kept:, :] = new_ref[...]

    @pl.when(jnp.logical_not(last))
    def _():
        o_ref[kept:, :] = next_ref[...]


def _roll_cache(cache, new_rows):
    depth, nb, length = cache.shape[:3]
    tokens = min(length, ROLL_TOKENS)
    rows = tokens * KV_ROWS
    flat = cache.reshape(depth, nb, length * KV_ROWS, HEAD_DIM)
    token = (None, None, KV_ROWS, HEAD_DIM)
    out = pl.pallas_call(
        _roll_kernel,
        grid=(depth, nb, length // tokens),
        in_specs=[pl.BlockSpec((None, None, rows, HEAD_DIM), lambda d, b, c: (d, b, c, 0)),
                  pl.BlockSpec(token, lambda d, b, c: (d, b, jnp.minimum((c + 1) * tokens, length - 1), 0)),
                  pl.BlockSpec(token, lambda d, b, c: (d, b, 0, 0))],
        out_specs=pl.BlockSpec((None, None, rows, HEAD_DIM), lambda d, b, c: (d, b, c, 0)),
        out_shape=jax.ShapeDtypeStruct(flat.shape, cache.dtype),
        compiler_params=_params(("parallel", "parallel", "arbitrary"), 2 * _nbytes((rows, HEAD_DIM), F32)),
        name="roll_cache",
    )(flat, flat, new_rows.reshape(depth, nb, KV_ROWS, HEAD_DIM))
    return out.reshape(cache.shape)


def _t5_bucket(dist):
    dist = np.asarray(dist)
    max_exact = NUM_BUCKETS // 2
    large = max_exact + (np.log(np.maximum(dist, 1) / max_exact) / np.log(REL_MAX_DIST / max_exact)
                         * (NUM_BUCKETS - max_exact)).astype(np.int32)
    large = np.minimum(large, NUM_BUCKETS - 1)
    return np.where(dist < max_exact, dist, large).astype(np.int32)


def _bias_tables(rel_bias):
    hpg = HEADS_PER_GROUP
    prompt, dec, dec0 = [], [], []
    for gi, (_, dil) in enumerate(DILATION_GROUPS):
        heads = rel_bias[:, gi * hpg:(gi + 1) * hpg].astype(F32)
        by_off = jnp.take(heads, _t5_bucket(np.arange(BLK + 1) * dil), axis=0).T
        w = jnp.concatenate([by_off[:, ::-1], jnp.full((hpg, BLK), NEG_INF, F32)], axis=1)
        tab = jnp.tile(w, (1, BLK))[:, :BLK * 2 * BLK].reshape(hpg, BLK, 2 * BLK)
        prompt.append(tab)
        dec.append(jnp.broadcast_to(by_off[:, :0:-1].T[:, :, None], (BLK, hpg, HEAD_DIM)))
        dec0.append(jnp.broadcast_to(by_off[:, 0:1], (hpg, HEAD_DIM)))
    return jnp.stack(prompt), jnp.stack(dec), jnp.stack(dec0)


def _prepare_weights(w_in, w_attn_br, w_sgu_br, w_out, w_ffn_gate, w_ffn_up, w_ffn_down,
                     w_ple_gate, w_ple_proj):
    return dict(
        w_in=w_in, w_gate=w_ffn_gate, w_up=w_ffn_up,
        w_attn_br=w_attn_br.astype(BF16), w_sgu_br=w_sgu_br.astype(BF16), w_out=w_out.astype(BF16),
        w_down=w_ffn_down.astype(BF16),
        w_ple_gate=w_ple_gate.astype(BF16), w_ple_proj=w_ple_proj.astype(BF16))


DOWN_TK = 2816


def _layer(prompt, sample, pp, ps, caches, wts, layer, depth, kv_prev, tables, g_mix_next, g_q, g_k,
           g_sgu, w_sgu, b_sgu, g_ffn, g_ple, batch, seq):
    (xp, h, inv), (xs, hs, invs) = prompt, sample
    tm, nb = 1024, xs.shape[0]
    bias_tab, bias_s, bias0_s = tables
    w_in = wts["w_in"]

    q_hm, zq = _q_proj(h, inv, hs, invs, w_in, layer, g_q, batch, seq, RES_TM)
    uv, zu = _proj_act(h, inv, hs, invs, w_in, layer, COL_U * TN // PTN, 2 * GM_WIDTH // PTN, "gelu",
                       RES_TM)
    gates, zg = _proj_act(h, inv, hs, invs, w_in, layer, COL_GATE * TN // PTN, 2 * D_MODEL // PTN,
                          "sigmoid", RES_TM)
    kv_hm, kvs, zkv = [], [], []
    for gi in range(N_GROUPS):
        hm, kv, zkv_g = _kv_proj(h, inv, hs, invs, w_in, layer, depth, g_k, gi, kv_prev[gi], batch, seq,
                                 RES_TM)
        kv_hm.append(hm)
        kvs.append(kv)
        zkv.append(zkv_g)

    attn_p = _attention(q_hm, kv_hm, bias_tab, batch, seq).reshape(batch * seq, ATTN_OUT)
    sgu_p = _sgu(uv, g_sgu, w_sgu, b_sgu, 512)
    z = jnp.concatenate([zq] + [zg_[:, :TN] for zg_ in zkv] + [zg_[:, TN:] for zg_ in zkv] + [zu, zg],
                        axis=1)
    kvn, attn_s, sguv, sgu_s, ga_s, gb_s = _sample_mix(z, caches, layer, g_q, g_k, g_sgu, w_sgu, b_sgu,
                                                       bias_s, bias0_s)

    wa, wb = wts["w_attn_br"], wts["w_sgu_br"]
    merged_p = _merge(attn_p, sgu_p, wa, wb, layer, gates, gates, D_MODEL // TN, BF16, tm)
    merged_s = _merge(attn_s.reshape(nb, ATTN_OUT), sgu_s.reshape(nb, GM_WIDTH), wa, wb, layer,
                      ga_s.reshape(nb, D_MODEL), gb_s.reshape(nb, D_MODEL), 0, F32, nb)
    xp, h, inv = _mm_res(merged_p, wts["w_out"], layer, xp, tm, TN // 2, D_MODEL, g_ffn, BF16)
    xs, hs, invs = _mm_res(merged_s, wts["w_out"], layer, xs, nb, TN, D_MODEL, g_ffn, F32)

    ff_p, ff_s = _glu(h, inv, hs, invs, wts["w_gate"], wts["w_up"], layer, tm)
    xp, h, inv = _mm_res(ff_p, wts["w_down"], layer, xp, tm, TN, DOWN_TK, g_ple, BF16)
    xs, hs, invs = _mm_res(ff_s, wts["w_down"], layer, xs, nb, TN, DOWN_TK, g_ple, F32)

    wg, wp = wts["w_ple_gate"], wts["w_ple_proj"]
    if g_mix_next is None:
        prompt = (_ple(h, inv, pp, wg, wp, layer, xp, tm), None, None)
        sample = (_ple(hs, invs, ps, wg, wp, layer, xs, nb), None, None)
    else:
        prompt = _ple(h, inv, pp, wg, wp, layer, xp, tm, g_mix_next, BF16)
        sample = _ple(hs, invs, ps, wg, wp, layer, xs, nb, g_mix_next, F32)
    return prompt, sample, kvs, kvn, sguv


def kernel(x_prompt, x_sample, cache_kv_w128, cache_kv_w512, cache_kv_w2048, p_prompt, p_sample,
           rel_bias, g_mix, w_in, g_q, g_k, g_sgu, w_sgu, b_sgu, w_attn_br, w_sgu_br, w_out, g_ffn,
           w_ffn_gate, w_ffn_up, w_ffn_down, g_ple, w_ple_gate, w_ple_proj):
    batch, seq, _ = x_prompt.shape
    nb = x_sample.shape[0]
    depth = w_in.shape[0]
    caches = (cache_kv_w128, cache_kv_w512, cache_kv_w2048)
    tables = _bias_tables(rel_bias)
    wts = _prepare_weights(w_in, w_attn_br, w_sgu_br, w_out, w_ffn_gate, w_ffn_up, w_ffn_down,
                           w_ple_gate, w_ple_proj)

    xp = x_prompt.reshape(batch * seq, D_MODEL)
    xs = x_sample.reshape(nb, D_MODEL)
    pp = p_prompt.reshape(depth, batch * seq, PLE_DIM)
    ps = p_sample.reshape(depth, nb, PLE_DIM)
    prompt = (xp, _rmsnorm(xp, g_mix[0], BF16, 256), jnp.ones((batch * seq, 1), F32))
    sample = (xs, _rmsnorm(xs, g_mix[0], F32, nb), jnp.ones((nb, 1), F32))
    kv_p = [None] * N_GROUPS
    kv_new = [[] for _ in range(N_GROUPS)]
    sgu_v = []
    for i in range(depth):
        g_mix_next = g_mix[i + 1] if i + 1 < depth else None
        params = (g_mix_next, g_q[i], g_k[i], g_sgu[i], w_sgu[i], b_sgu[i], g_ffn[i], g_ple[i])
        prompt, sample, kv_p, kvn, sguv = _layer(prompt, sample, pp, ps, caches, wts, i, depth, kv_p,
                                                 tables, *params, batch, seq)
        for gi in range(N_GROUPS):
            kv_new[gi].append(kvn[:, gi])
        sgu_v.append(sguv)

    xp, xs = prompt[0], sample[0]
    kv_p = [kv.reshape(depth, batch, -1, 2, HEADS_PER_GROUP, HEAD_DIM) for kv in kv_p]
    kv_s = [_roll_cache(c, jnp.stack(kv_new[gi], axis=0)) for gi, c in enumerate(caches)]
    return (xp.reshape(batch, seq, D_MODEL), xs.reshape(nb, 1, D_MODEL), *kv_p, *kv_s,
            jnp.stack(sgu_v, axis=0))
```

```python
import functools

import numpy as np
import jax
import jax.numpy as jnp
from jax import lax
from jax.experimental import pallas as pl
from jax.experimental.pallas import tpu as pltpu

D_MODEL = 4096
HEAD_DIM = 128
DILATION_GROUPS = ((128, 1), (512, 4), (2048, 16))
N_GROUPS = 3
HEADS_PER_GROUP = 8
N_ATTN_HEADS = 24
ATTN_WIDTH = 3072
ATTN_OUT = 1024
BLK = 128
KV_ROWS = 2 * HEADS_PER_GROUP
GM_WIDTH = 2048
GM_GROUPS = 16
PLE_DIM = 256
NUM_BUCKETS = 32
REL_MAX_DIST = 2048
NORM_EPS = 1e-6
NEG_INF = -1e30
Q_SCALE = HEAD_DIM ** -0.5

TN = 1024
COL_U, COL_GATE = 9, 13

V7X_VMEM_REQUEST_CAP = 60000 * 1024
COMPILER_TEMP_BYTES = 16 * 1024 * 1024

F32 = jnp.float32
BF16 = jnp.bfloat16


def _params(sems, window_bytes, scratch_bytes=0):
    limit = min(V7X_VMEM_REQUEST_CAP, 2 * window_bytes + scratch_bytes + COMPILER_TEMP_BYTES)
    return pltpu.CompilerParams(dimension_semantics=sems, vmem_limit_bytes=int(limit))


def _nbytes(shape, dtype):
    return int(np.prod(shape)) * jnp.dtype(dtype).itemsize


SQRT_2_OVER_PI = 0.7978845608028654
GELU_CUBIC = 0.044715


def _gelu(x):
    return x * (0.5 * (1.0 + jnp.tanh(SQRT_2_OVER_PI * (x + GELU_CUBIC * (x * x * x)))))


def _silu(x):
    return x * jax.nn.sigmoid(x)


_ACT = {"gelu": _gelu, "sigmoid": jax.nn.sigmoid}


def _rms(x):
    return x * lax.rsqrt(jnp.mean(x * x, axis=-1, keepdims=True) + NORM_EPS)


def _dot(a, b):
    return jnp.dot(a.astype(BF16), b.astype(BF16), preferred_element_type=F32)


def _dot_nt(a, b):
    return lax.dot_general(a, b, (((1,), (1,)), ((), ())), preferred_element_type=F32)


def _wspec(layer, k, tn, col_map):
    return pl.BlockSpec((None, k, tn), lambda *ids: (layer, 0, col_map(*ids)))


def _rmsnorm_kernel(x_ref, g_ref, o_ref):
    o_ref[...] = (_rms(x_ref[...]) * g_ref[...]).astype(o_ref.dtype)


def _rmsnorm(x, g, out_dtype, tm):
    m, d = x.shape
    return pl.pallas_call(
        _rmsnorm_kernel,
        grid=(m // tm,),
        in_specs=[pl.BlockSpec((tm, d), lambda i: (i, 0)), pl.BlockSpec((1, d), lambda i: (0, 0))],
        out_specs=pl.BlockSpec((tm, d), lambda i: (i, 0)),
        out_shape=jax.ShapeDtypeStruct((m, d), out_dtype),
        compiler_params=_params(("parallel",), _nbytes((tm, d), F32) + _nbytes((tm, d), out_dtype)),
        name="rmsnorm",
    )(x, g.reshape(1, d))


PTN = 1024
RES_TM = 512


def _resident_cast(w_refs, wb_refs):
    for w_ref, wb_ref in zip(w_refs, wb_refs):
        wb_ref[...] = w_ref[...].astype(BF16)


def _row_specs(tm, k, nb):
    return [pl.BlockSpec((tm, k), lambda j, i: (i, 0)), pl.BlockSpec((tm, 1), lambda j, i: (i, 0)),
            pl.BlockSpec((nb, k), lambda j, i: (0, 0)), pl.BlockSpec((nb, 1), lambda j, i: (0, 0))]


def _proj_act_kernel(x_ref, inv_ref, xs_ref, invs_ref, w_ref, o_ref, os_ref, wb, *, act):
    @pl.when(pl.program_id(1) == 0)
    def _():
        _resident_cast([w_ref], [wb])
        os_ref[...] = _dot(xs_ref[...], wb[...]) * invs_ref[...]

    o_ref[...] = _ACT[act](_dot(x_ref[...], wb[...]) * inv_ref[...]).astype(o_ref.dtype)


def _proj_act(h, inv, hs, invs, w, layer, col0, n_tiles, act, tm):
    (m, k), nb = h.shape, hs.shape[0]
    win = _nbytes((tm, k), BF16) + _nbytes((k, PTN), F32) + _nbytes((tm, PTN), BF16)
    return pl.pallas_call(
        functools.partial(_proj_act_kernel, act=act),
        grid=(n_tiles, m // tm),
        in_specs=[*_row_specs(tm, k, nb), _wspec(layer, k, PTN, lambda j, i: j + col0)],
        out_specs=[pl.BlockSpec((tm, PTN), lambda j, i: (i, j)),
                   pl.BlockSpec((nb, PTN), lambda j, i: (0, j))],
        out_shape=[jax.ShapeDtypeStruct((m, n_tiles * PTN), BF16),
                   jax.ShapeDtypeStruct((nb, n_tiles * PTN), F32)],
        scratch_shapes=[pltpu.VMEM((k, PTN), BF16)],
        compiler_params=_params(("arbitrary", "arbitrary"), win, _nbytes((k, PTN), BF16)),
        name="proj_" + act,
    )(h, inv, hs, invs, w)


def _q_kernel(x_ref, inv_ref, xs_ref, invs_ref, w_ref, g_ref, o_ref, os_ref, wb):
    @pl.when(pl.program_id(1) == 0)
    def _():
        _resident_cast([w_ref], [wb])
        os_ref[...] = _dot(xs_ref[...], wb[...]) * invs_ref[...]

    acc = _dot(x_ref[...], wb[...]) * inv_ref[...]
    for hh in range(PTN // HEAD_DIM):
        a = acc[:, hh * HEAD_DIM:(hh + 1) * HEAD_DIM]
        o_ref[0, hh] = ((_rms(a) * g_ref[...]) * Q_SCALE).astype(o_ref.dtype)


def _q_proj(h, inv, hs, invs, w, layer, g_q, batch, seq, tm):
    (m, k), nb = h.shape, hs.shape[0]
    tpb = seq // tm
    heads = PTN // HEAD_DIM
    win = _nbytes((tm, k), BF16) + _nbytes((k, PTN), F32) + _nbytes((tm, PTN), BF16)
    return pl.pallas_call(
        _q_kernel,
        grid=(ATTN_WIDTH // PTN, m // tm),
        in_specs=[*_row_specs(tm, k, nb), _wspec(layer, k, PTN, lambda j, i: j),
                  pl.BlockSpec((1, HEAD_DIM), lambda j, i: (0, 0))],
        out_specs=[pl.BlockSpec((1, heads, tm, HEAD_DIM), lambda j, i: (i // tpb, j, i % tpb, 0)),
                   pl.BlockSpec((nb, PTN), lambda j, i: (0, j))],
        out_shape=[jax.ShapeDtypeStruct((batch, N_ATTN_HEADS, seq, HEAD_DIM), BF16),
                   jax.ShapeDtypeStruct((nb, ATTN_WIDTH), F32)],
        scratch_shapes=[pltpu.VMEM((k, PTN), BF16)],
        compiler_params=_params(("arbitrary", "arbitrary"), win, _nbytes((k, PTN), BF16)),
        name="q_proj",
    )(h, inv, hs, invs, w, g_q.reshape(1, HEAD_DIM))


def _kv_kernel(x_ref, inv_ref, xs_ref, invs_ref, w_ref, g_ref, kv_prev_ref, hm_ref, kv_ref, os_ref, wb,
               *, rows):
    del kv_prev_ref

    @pl.when(pl.program_id(1) == 0)
    def _():
        _resident_cast([w_ref], [wb])
        os_ref[...] = _dot(xs_ref[...], wb[...]) * invs_ref[...]

    acc = _dot(x_ref[...], wb[...]) * inv_ref[...]
    tm = acc.shape[0]
    is_key = pl.program_id(0) == 0
    kv_rows = kv_ref.reshape(rows * HEADS_PER_GROUP, HEAD_DIM)
    for hh in range(HEADS_PER_GROUP):
        a = acc[:, hh * HEAD_DIM:(hh + 1) * HEAD_DIM]
        y = jnp.where(is_key, _rms(a) * g_ref[...], a)
        hm_ref[0, hh] = y.astype(BF16)
        kv_rows[pl.ds(hh, rows, stride=HEADS_PER_GROUP), :] = y[tm - rows:]


def _kv_proj(h, inv, hs, invs, w, layer, depth, g_k, gi, kv_prev, batch, seq, tm):
    (m, k), nb = h.shape, hs.shape[0]
    tpb = seq // tm
    keep = min(DILATION_GROUPS[gi][0], seq)
    rows = min(keep, tm)
    kv_tiles = keep // rows
    hpg = HEADS_PER_GROUP
    kv_map = lambda j, i: (layer, i // tpb, jnp.maximum(i % tpb - (tpb - kv_tiles), 0), j, 0)
    kv_shape = jax.ShapeDtypeStruct((depth, batch, keep, KV_ROWS, HEAD_DIM), F32)
    win = _nbytes((tm, k), BF16) + _nbytes((tm, TN), BF16) + _nbytes((rows, hpg, HEAD_DIM), F32)
    resident = _nbytes((k, TN), F32) + _nbytes((k, TN), BF16)
    if kv_prev is None:
        kv_prev = jnp.zeros(kv_shape.shape, F32)
    return pl.pallas_call(
        functools.partial(_kv_kernel, rows=rows),
        grid=(2, m // tm),
        in_specs=[*_row_specs(tm, k, nb),
                  pl.BlockSpec((None, k, TN), lambda j, i: (layer, 0, ATTN_WIDTH // TN * (1 + j) + gi),
                               pipeline_mode=pl.Buffered(1)),
                  pl.BlockSpec((1, HEAD_DIM), lambda j, i: (0, 0)),
                  pl.BlockSpec(memory_space=pl.ANY)],
        out_specs=[pl.BlockSpec((None, 1, hpg, tm, HEAD_DIM), lambda j, i: (j, i // tpb, 0, i % tpb, 0)),
                   pl.BlockSpec((None, None, rows, hpg, HEAD_DIM), kv_map),
                   pl.BlockSpec((nb, TN), lambda j, i: (0, j))],
        out_shape=[jax.ShapeDtypeStruct((2, batch, hpg, seq, HEAD_DIM), BF16), kv_shape,
                   jax.ShapeDtypeStruct((nb, 2 * TN), F32)],
        scratch_shapes=[pltpu.VMEM((k, TN), BF16)],
        input_output_aliases={6: 1},
        compiler_params=_params(("arbitrary", "arbitrary"), win, resident),
        name="kv_proj",
    )(h, inv, hs, invs, w, g_k.reshape(1, HEAD_DIM), kv_prev)


ATTN_BATCH = 8


def _attn_blocks(qkv, bias):
    scores = [_dot_nt(q, k) + bias for q, k, _ in qkv]
    tops = [jnp.max(s, axis=-1, keepdims=True) for s in scores]
    probs = [jnp.exp(s - m).astype(BF16) for s, m in zip(scores, tops)]
    nds = [jnp.dot(e, jnp.concatenate([v, jnp.ones_like(v)], axis=1), preferred_element_type=F32)
           for e, (_, _, v) in zip(probs, qkv)]
    return [(nd[:, :HEAD_DIM] / nd[:, HEAD_DIM:], m + jnp.log(nd[:, HEAD_DIM:])) for nd, m in zip(nds, tops)]


def _attn_kernel(q0_ref, q1_ref, q2_ref, k0_ref, k1_ref, k2_ref, v0_ref, v1_ref, v2_ref, bias_ref, o_ref,
                 q1f, q2f, k1f, k2f, v1f, v2f, o0_sc, o1_sc, o2_sc, l0_sc, l1_sc, l2_sc, *, seq):
    for src, dst in ((q1_ref, q1f), (q2_ref, q2f), (k1_ref, k1f), (k2_ref, k2f), (v1_ref, v1f), (v2_ref, v2f)):
        dst[...] = src[0, 0].astype(F32)

    def run_group(gi, load, o_sc, l_sc, unroll):
        dil = DILATION_GROUPS[gi][1]
        nblk = seq // dil // BLK

        def rows(r, s0, n):
            if dil == 1:
                return pl.ds(s0 if isinstance(s0, int) else pl.multiple_of(s0, BLK), n)
            return pl.ds(s0 * dil + r, n, stride=dil)

        def blocks(todo, first):
            back = 0 if first else BLK
            bias = bias_ref[gi, 0, :, BLK:2 * BLK] if first else bias_ref[gi, 0]
            qkv = [load(rows(r, s0, BLK), rows(r, s0 - back, BLK + back)) for r, s0 in todo]
            for (r, s0), (o, lse) in zip(todo, _attn_blocks(qkv, bias)):
                o_sc[rows(r, s0, BLK), :] = o
                l_sc[rows(r, s0, BLK), :] = lse

        for r0 in range(0, dil, ATTN_BATCH):
            blocks([(r, 0) for r in range(r0, min(dil, r0 + ATTN_BATCH))], True)

        def body(it, carry):
            blocks([(r, (1 + it * unroll + u) * BLK) for u in range(unroll) for r in range(dil)], False)
            return carry

        if nblk > 1:
            lax.fori_loop(0, (nblk - 1) // unroll, body, 0)

    def bf16_load(q_ref, k_ref, v_ref):
        return lambda qr, kr: (q_ref[0, 0, qr, :], k_ref[0, 0, kr, :], v_ref[0, 0, kr, :])

    def f32_load(qf, kf, vf):
        return lambda qr, kr: (qf[qr, :].astype(BF16), kf[kr, :].astype(BF16), vf[kr, :].astype(BF16))

    run_group(0, bf16_load(q0_ref, k0_ref, v0_ref), o0_sc, l0_sc, 5)
    run_group(1, f32_load(q1f, k1f, v1f), o1_sc, l1_sc, 1)
    run_group(2, f32_load(q2f, k2f, v2f), o2_sc, l2_sc, 1)

    def combine(c, carry):
        rows = pl.ds(pl.multiple_of(c * BLK, BLK), BLK)
        l0, l1, l2 = l0_sc[rows, :], l1_sc[rows, :], l2_sc[rows, :]
        top = jnp.maximum(jnp.maximum(l0, l1), l2)
        w0, w1, w2 = jnp.exp(l0 - top), jnp.exp(l1 - top), jnp.exp(l2 - top)
        mixed = (w0 * o0_sc[rows, :] + w1 * o1_sc[rows, :] + w2 * o2_sc[rows, :]) / (w0 + w1 + w2)
        o_ref[0, rows, :] = mixed.astype(o_ref.dtype)
        return carry

    lax.fori_loop(0, seq // BLK, combine, 0)


def _attention(q_hm, kv_hm, bias_tab, batch, seq):
    slab = (1, 1, seq, HEAD_DIM)
    specs = [pl.BlockSpec(slab, lambda b, h, gi=gi: (b, gi * HEADS_PER_GROUP + h, 0, 0))
             for gi in range(N_GROUPS)]
    for which in range(2):
        specs += [pl.BlockSpec((None,) + slab, lambda b, h, which=which: (which, b, h, 0, 0))] * N_GROUPS
    specs.append(pl.BlockSpec((N_GROUPS, 1, BLK, 2 * BLK), lambda b, h: (0, h, 0, 0)))
    win = (10 * _nbytes((seq, HEAD_DIM), BF16) + _nbytes((N_GROUPS, BLK, 2 * BLK), F32)
           + 6 * _nbytes((seq, HEAD_DIM), F32))
    return pl.pallas_call(
        functools.partial(_attn_kernel, seq=seq),
        grid=(batch, HEADS_PER_GROUP),
        in_specs=specs,
        out_specs=pl.BlockSpec((1, seq, HEAD_DIM), lambda b, h: (b, 0, h)),
        out_shape=jax.ShapeDtypeStruct((batch, seq, ATTN_OUT), BF16),
        scratch_shapes=[pltpu.VMEM((seq, HEAD_DIM), F32)] * 12,
        compiler_params=_params(("parallel", "arbitrary"), win),
        name="dilated_attention",
    )(q_hm, q_hm, q_hm, *kv_hm, *kv_hm, bias_tab)


def _sgu_kernel(u_ref, gv_ref, g_ref, w_ref, bt_ref, o_ref, vs_sc):
    gv = gv_ref[...].astype(F32)
    vs_sc[...] = (_rms(gv) * g_ref[...]).astype(BF16)
    tm = gv.shape[0]
    row = lax.broadcasted_iota(jnp.int32, (BLK, BLK), 0)
    col = lax.broadcasted_iota(jnp.int32, (BLK, BLK), 1)
    tril = (row >= col).astype(F32)
    for g in range(GM_GROUPS):
        cols = slice(g * BLK, (g + 1) * BLK)
        wg = (w_ref[g] * tril).astype(BF16)
        bg = bt_ref[:, g:g + 1]
        for c in range(tm // BLK):
            rows = slice(c * BLK, (c + 1) * BLK)
            mix = _dot(wg, vs_sc[rows, cols]) + bg
            o_ref[rows, cols] = (u_ref[rows, cols].astype(F32) * mix).astype(o_ref.dtype)


def _sgu(uv, g_sgu, w_sgu, b_sgu, tm):
    m, c = uv.shape[0], GM_WIDTH
    win = 3 * _nbytes((tm, c), BF16) + _nbytes((GM_GROUPS, BLK, BLK), F32)
    return pl.pallas_call(
        _sgu_kernel,
        grid=(m // tm,),
        in_specs=[pl.BlockSpec((tm, c), lambda i: (i, 0)),
                  pl.BlockSpec((tm, c), lambda i: (i, 1)),
                  pl.BlockSpec((1, c), lambda i: (0, 0)),
                  pl.BlockSpec((GM_GROUPS, BLK, BLK), lambda i: (0, 0, 0)),
                  pl.BlockSpec((BLK, GM_GROUPS), lambda i: (0, 0))],
        out_specs=pl.BlockSpec((tm, c), lambda i: (i, 0)),
        out_shape=jax.ShapeDtypeStruct((m, c), BF16),
        scratch_shapes=[pltpu.VMEM((tm, c), BF16)],
        compiler_params=_params(("parallel",), win),
        name="spatial_gating",
    )(uv, uv, g_sgu.reshape(1, c), w_sgu, b_sgu.T)


def _merge_kernel(a_ref, s_ref, wa_ref, wb_ref, ga_ref, gb_ref, o_ref):
    a = _dot(a_ref[...], wa_ref[...])
    b = _dot(s_ref[...], wb_ref[...])
    o_ref[...] = (ga_ref[...].astype(F32) * a + gb_ref[...].astype(F32) * b).astype(o_ref.dtype)


def _merge(attn, sgu, wa, wb, layer, ga, gb, gb_col0, out_dtype, tm):
    m = attn.shape[0]
    win = (_nbytes((tm, ATTN_OUT), attn.dtype) + _nbytes((tm, GM_WIDTH), sgu.dtype)
           + _nbytes((ATTN_OUT + GM_WIDTH, TN), BF16) + 2 * _nbytes((tm, TN), ga.dtype)
           + _nbytes((tm, TN), out_dtype))
    row = lambda width: pl.BlockSpec((tm, width), lambda i, j: (i, 0))
    tile = pl.BlockSpec((tm, TN), lambda i, j: (i, j))
    return pl.pallas_call(
        _merge_kernel,
        grid=(m // tm, D_MODEL // TN),
        in_specs=[row(ATTN_OUT), row(GM_WIDTH),
                  _wspec(layer, ATTN_OUT, TN, lambda i, j: j),
                  _wspec(layer, GM_WIDTH, TN, lambda i, j: j), tile,
                  pl.BlockSpec((tm, TN), lambda i, j: (i, j + gb_col0))],
        out_specs=tile,
        out_shape=jax.ShapeDtypeStruct((m, D_MODEL), out_dtype),
        compiler_params=_params(("parallel", "arbitrary"), win),
        name="branch_merge",
    )(attn, sgu, wa, wb, ga, gb)


def _emit_scaled(y, g_ref, xg_ref, inv_ref, ss_sc, j, n_col_tiles):
    xg_ref[...] = (y * g_ref[...]).astype(xg_ref.dtype)
    ss = jnp.sum(y * y, axis=-1, keepdims=True)

    @pl.when(j == 0)
    def _():
        ss_sc[...] = ss

    @pl.when(j > 0)
    def _():
        ss_sc[...] += ss

    @pl.when(j == n_col_tiles - 1)
    def _():
        inv_ref[...] = lax.rsqrt(ss_sc[...] * (1.0 / (n_col_tiles * y.shape[1])) + NORM_EPS)


def _norm_out(m, n, tm, tn, xg_dtype, col_of):
    in_spec = pl.BlockSpec((1, tn), lambda *ids: (0, col_of(*ids)))
    out_specs = [pl.BlockSpec((tm, tn), lambda *ids: (ids[0], col_of(*ids))),
                 pl.BlockSpec((tm, 1), lambda *ids: (ids[0], 0))]
    out_shape = [jax.ShapeDtypeStruct((m, n), xg_dtype), jax.ShapeDtypeStruct((m, 1), F32)]
    return in_spec, out_specs, out_shape, pltpu.VMEM((tm, 1), F32)


def _mm_res_kernel(x_ref, w_ref, r_ref, *rest, k_rows, norm, grid):
    o_ref = rest[1] if norm else rest[0]
    x, w = x_ref[...], w_ref[...]
    tk = w.shape[0]
    if k_rows % tk:
        k0 = pl.program_id(2) * tk
        w = jnp.where(k0 + lax.broadcasted_iota(jnp.int32, w.shape, 0) < k_rows, w, jnp.zeros_like(w))
        x = jnp.where(k0 + lax.broadcasted_iota(jnp.int32, x.shape, 1) < k_rows, x, jnp.zeros_like(x))
    part = _dot(x, w)

    @pl.when(pl.program_id(2) == 0)
    def _():
        o_ref[...] = r_ref[...] + part

    @pl.when(pl.program_id(2) > 0)
    def _():
        o_ref[...] += part

    if norm:
        g_ref, _, xg_ref, inv_ref, ss_sc = rest

        @pl.when(pl.program_id(2) == grid[2] - 1)
        def _():
            _emit_scaled(o_ref[...], g_ref, xg_ref, inv_ref, ss_sc, pl.program_id(1), grid[1])


def _mm_res(x, w, layer, res, tm, tn, tk, g_next=None, xg_dtype=None):
    m, k = x.shape
    n = w.shape[2]
    norm = g_next is not None
    win = _nbytes((tm, tk), x.dtype) + _nbytes((tk, tn), BF16) + 2 * _nbytes((tm, tn), F32)
    ins = [x, w, res]
    in_specs = [pl.BlockSpec((tm, tk), lambda i, j, kk: (i, kk)),
                pl.BlockSpec((None, tk, tn), lambda i, j, kk: (layer, kk, j)),
                pl.BlockSpec((tm, tn), lambda i, j, kk: (i, j))]
    out_specs = [pl.BlockSpec((tm, tn), lambda i, j, kk: (i, j))]
    out_shape = [jax.ShapeDtypeStruct((m, n), F32)]
    scratch = []
    if norm:
        g_spec, n_specs, n_shapes, ss = _norm_out(m, n, tm, tn, xg_dtype, lambda i, j, kk: j)
        ins.append(g_next.reshape(1, n))
        in_specs.append(g_spec)
        out_specs += n_specs
        out_shape += n_shapes
        scratch.append(ss)
        win += _nbytes((tm, tn), xg_dtype)
    grid = (m // tm, n // tn, pl.cdiv(k, tk))
    out = pl.pallas_call(
        functools.partial(_mm_res_kernel, k_rows=k, norm=norm, grid=grid),
        grid=grid,
        in_specs=in_specs,
        out_specs=out_specs,
        out_shape=out_shape,
        scratch_shapes=scratch,
        compiler_params=_params(("parallel", "arbitrary", "arbitrary"), win),
        name="matmul_residual",
    )(*ins)
    return out if norm else out[0]


GLU_TN = 256


def _glu_row_copy(x_hbm, xbuf, sems, i, tm):
    slot = i % 2
    return pltpu.make_async_copy(x_hbm.at[pl.ds(i * tm, tm), :], xbuf.at[slot], sems.at[slot])


def _glu_kernel(inv_ref, xs_ref, invs_ref, wg_ref, wu_ref, x_hbm, o_ref, os_ref, wbg, wbu, xbuf, sems, *, tm):
    j, nj = pl.program_id(0), pl.num_programs(0)
    n_row_tiles = o_ref.shape[0] // tm
    copy = functools.partial(_glu_row_copy, x_hbm, xbuf, sems, tm=tm)

    @pl.when(j == 0)
    def _():
        copy(0).start()

    _resident_cast([wg_ref, wu_ref], [wbg, wbu])
    xs, invs = xs_ref[...], invs_ref[...]
    os_ref[...] = _silu(_dot(xs, wbg[...]) * invs) * (_dot(xs, wbu[...]) * invs)

    for i in range(n_row_tiles):
        copy(i).wait()
        if i + 1 < n_row_tiles:
            copy(i + 1).start()
        else:
            @pl.when(j + 1 < nj)
            def _():
                copy(0).start()

        rows = slice(i * tm, (i + 1) * tm)
        x, inv = xbuf[i % 2], inv_ref[rows, :]
        o_ref[rows, :] = (_silu(_dot(x, wbg[...]) * inv) * (_dot(x, wbu[...]) * inv)).astype(o_ref.dtype)


def _glu(h, inv, hs, invs, wg, wu, layer, tm):
    (m, k), nb = h.shape, hs.shape[0]
    n = wg.shape[2]
    assert (m // tm) % 2 == 0, "tile 0 must return to ring slot 0 on the next step"
    win = 2 * _nbytes((k, GLU_TN), F32) + _nbytes((m, GLU_TN), BF16)
    scratch = 2 * _nbytes((k, GLU_TN), BF16) + 2 * _nbytes((tm, k), BF16) + _nbytes((m, HEAD_DIM), F32)
    wspec = _wspec(layer, k, GLU_TN, lambda j: j)
    return pl.pallas_call(
        functools.partial(_glu_kernel, tm=tm),
        grid=(n // GLU_TN,),
        in_specs=[pl.BlockSpec((m, 1), lambda j: (0, 0), pipeline_mode=pl.Buffered(1)),
                  pl.BlockSpec((nb, k), lambda j: (0, 0)), pl.BlockSpec((nb, 1), lambda j: (0, 0)),
                  wspec, wspec, pl.BlockSpec(memory_space=pl.ANY)],
        out_specs=[pl.BlockSpec((m, GLU_TN), lambda j: (0, j)),
                   pl.BlockSpec((nb, GLU_TN), lambda j: (0, j))],
        out_shape=[jax.ShapeDtypeStruct((m, n), BF16), jax.ShapeDtypeStruct((nb, n), F32)],
        scratch_shapes=[pltpu.VMEM((k, GLU_TN), BF16), pltpu.VMEM((k, GLU_TN), BF16),
                        pltpu.VMEM((2, tm, k), BF16), pltpu.SemaphoreType.DMA((2,))],
        compiler_params=_params(("arbitrary",), win, scratch),
        name="swiglu_up",
    )(inv, hs, invs, wg, wu, h)


PLE_TN = 512


def _ple_kernel(h_ref, inv_ref, p_ref, wg_ref, wp_ref, r_ref, *rest, norm, grid):
    o_ref = rest[1] if norm else rest[0]
    gate = jax.nn.sigmoid(_dot(h_ref[...], wg_ref[...]) * inv_ref[...])
    y = r_ref[...] + gate * _dot(p_ref[...], wp_ref[...])
    o_ref[...] = y
    if norm:
        g_ref, _, xg_ref, invn_ref, ss_sc = rest
        _emit_scaled(y, g_ref, xg_ref, invn_ref, ss_sc, pl.program_id(1), grid[1])


def _ple(h, inv, p, wg, wp, layer, res, tm, g_next=None, xg_dtype=None):
    m, k = h.shape
    tn = PLE_TN
    norm = g_next is not None
    win = (_nbytes((tm, k), h.dtype) + _nbytes((tm, PLE_DIM), F32) + _nbytes((k + PLE_DIM, tn), BF16)
           + 2 * _nbytes((tm, tn), F32))
    tile = pl.BlockSpec((tm, tn), lambda i, j: (i, j))
    ins = [h, inv, p, wg, wp, res]
    in_specs = [pl.BlockSpec((tm, k), lambda i, j: (i, 0)),
                pl.BlockSpec((tm, 1), lambda i, j: (i, 0)),
                pl.BlockSpec((None, tm, PLE_DIM), lambda i, j: (layer, i, 0)),
                _wspec(layer, k, tn, lambda i, j: j),
                _wspec(layer, PLE_DIM, tn, lambda i, j: j), tile]
    out_specs = [tile]
    out_shape = [jax.ShapeDtypeStruct((m, D_MODEL), F32)]
    scratch = []
    if norm:
        g_spec, n_specs, n_shapes, ss = _norm_out(m, D_MODEL, tm, tn, xg_dtype, lambda i, j: j)
        ins.append(g_next.reshape(1, D_MODEL))
        in_specs.append(g_spec)
        out_specs += n_specs
        out_shape += n_shapes
        scratch.append(ss)
        win += _nbytes((tm, tn), xg_dtype)
    grid = (m // tm, D_MODEL // tn)
    out = pl.pallas_call(
        functools.partial(_ple_kernel, norm=norm, grid=grid),
        grid=grid,
        in_specs=in_specs,
        out_specs=out_specs,
        out_shape=out_shape,
        scratch_shapes=scratch,
        compiler_params=_params(("parallel", "arbitrary"), win),
        name="ple_gate",
    )(*ins)
    return out if norm else out[0]


def _sample_mix_kernel(zh_ref, zr_ref, c0_ref, c1_ref, c2_ref, gq_ref, gk_ref, gsgu_ref,
                       w00_ref, b0_ref, bias_ref, bias0_ref,
                       kvn_ref, attn_ref, sguv_ref, sguo_ref, ga_ref, gb_ref, qn_sc, kn_sc):
    nh, hpg = N_ATTN_HEADS, HEADS_PER_GROUP
    qn_sc[...] = (_rms(zh_ref[0, 0:nh, :]) * gq_ref[...]) * Q_SCALE
    kn_sc[...] = _rms(zh_ref[0, nh:2 * nh, :]) * gk_ref[...]

    outs, lses = [], []
    for gi, c_ref in enumerate((c0_ref, c1_ref, c2_ref)):
        hs = slice(gi * hpg, (gi + 1) * hpg)
        q, k_new = qn_sc[hs, :], kn_sc[hs, :]
        v_new = zh_ref[0, 2 * nh + gi * hpg:2 * nh + (gi + 1) * hpg, :]
        kvn_ref[0, gi, 0] = k_new
        kvn_ref[0, gi, 1] = v_new
        kc = c_ref[0, 0, :, 0:hpg, :]
        vc = c_ref[0, 0, :, hpg:2 * hpg, :]
        s = jnp.sum(kc * q[None], axis=-1, keepdims=True) + bias_ref[gi]
        s_new = jnp.sum(q * k_new, axis=-1, keepdims=True) + bias0_ref[gi]
        m = jnp.maximum(jnp.max(s, axis=0), s_new)
        e = jnp.exp(s - m[None])
        e_new = jnp.exp(s_new - m)
        den = jnp.sum(e, axis=0) + e_new
        outs.append((jnp.sum(e * vc, axis=0) + e_new * v_new) / den)
        lses.append(m + jnp.log(den))
    top = jnp.maximum(jnp.maximum(lses[0], lses[1]), lses[2])
    wts = [jnp.exp(l - top) for l in lses]
    attn_ref[0] = (wts[0] * outs[0] + wts[1] * outs[1] + wts[2] * outs[2]) / (wts[0] + wts[1] + wts[2])

    c = GM_WIDTH
    u = _gelu(zr_ref[0, :, 0:c])
    vs = _rms(_gelu(zr_ref[0, :, c:2 * c])) * gsgu_ref[...]
    sguv_ref[0] = vs
    sguo_ref[0] = u * (w00_ref[...] * vs + b0_ref[...])
    ga_ref[0] = jax.nn.sigmoid(zr_ref[0, :, 2 * c:2 * c + D_MODEL])
    gb_ref[0] = jax.nn.sigmoid(zr_ref[0, :, 2 * c + D_MODEL:2 * c + 2 * D_MODEL])


def _sample_mix(z, caches, layer, g_q, g_k, g_sgu, w_sgu, b_sgu, bias_s, bias0_s):
    nb = z.shape[0]
    hpg = HEADS_PER_GROUP
    zh = z[:, :3 * ATTN_WIDTH].reshape(nb, 3 * N_ATTN_HEADS, HEAD_DIM)
    rest = 2 * GM_WIDTH + 2 * D_MODEL
    zr = z[:, 3 * ATTN_WIDTH:].reshape(nb, 1, rest)
    w00 = jnp.repeat(w_sgu[:, 0, 0], BLK).reshape(1, GM_WIDTH)
    b0 = jnp.repeat(b_sgu[:, 0], BLK).reshape(1, GM_WIDTH)
    cache_views, cache_specs = [], []
    for c, (_, dil) in zip(caches, DILATION_GROUPS):
        depth, _, length = c.shape[:3]
        cache_views.append(c.reshape(depth, nb, length // dil, dil * KV_ROWS, HEAD_DIM))
        cache_specs.append(pl.BlockSpec((1, 1, BLK, KV_ROWS, HEAD_DIM), lambda b: (layer, b, 0, 0, 0)))
    vec = lambda width: pl.BlockSpec((1, width), lambda b: (0, 0))
    per_b = lambda width: pl.BlockSpec((1, 1, width), lambda b: (b, 0, 0))
    win = (3 * _nbytes((BLK, KV_ROWS, HEAD_DIM), F32) + _nbytes((1, 4 * rest), F32)
           + _nbytes((N_GROUPS, BLK + 1, hpg, HEAD_DIM), F32))
    return pl.pallas_call(
        _sample_mix_kernel,
        grid=(nb,),
        in_specs=[pl.BlockSpec((1, 3 * N_ATTN_HEADS, HEAD_DIM), lambda b: (b, 0, 0)), per_b(rest),
                  *cache_specs, vec(HEAD_DIM), vec(HEAD_DIM), vec(GM_WIDTH), vec(GM_WIDTH),
                  vec(GM_WIDTH),
                  pl.BlockSpec((N_GROUPS, BLK, hpg, HEAD_DIM), lambda b: (0, 0, 0, 0)),
                  pl.BlockSpec((N_GROUPS, hpg, HEAD_DIM), lambda b: (0, 0, 0))],
        out_specs=[pl.BlockSpec((1, N_GROUPS, 2, hpg, HEAD_DIM), lambda b: (b, 0, 0, 0, 0)),
                   pl.BlockSpec((1, hpg, HEAD_DIM), lambda b: (b, 0, 0)),
                   per_b(GM_WIDTH), per_b(GM_WIDTH), per_b(D_MODEL), per_b(D_MODEL)],
        out_shape=[jax.ShapeDtypeStruct((nb, N_GROUPS, 2, hpg, HEAD_DIM), F32),
                   jax.ShapeDtypeStruct((nb, hpg, HEAD_DIM), F32),
                   jax.ShapeDtypeStruct((nb, 1, GM_WIDTH), F32),
                   jax.ShapeDtypeStruct((nb, 1, GM_WIDTH), F32),
                   jax.ShapeDtypeStruct((nb, 1, D_MODEL), F32),
                   jax.ShapeDtypeStruct((nb, 1, D_MODEL), F32)],
        scratch_shapes=[pltpu.VMEM((N_ATTN_HEADS, HEAD_DIM), F32)] * 2,
        compiler_params=_params(("parallel",), win),
        name="decode_mixers",
    )(zh, zr, *cache_views, g_q.reshape(1, HEAD_DIM), g_k.reshape(1, HEAD_DIM),
      g_sgu.reshape(1, GM_WIDTH), w00, b0, bias_s, bias0_s)


ROLL_TOKENS = 1024


def _roll_kernel(chunk_ref, next_ref, new_ref, o_ref):
    kept = chunk_ref.shape[0] - KV_ROWS
    o_ref[0:kept, :] = chunk_ref[KV_ROWS:, :]
    last = pl.program_id(2) == pl.num_programs(2) - 1

    @pl.when(last)
    def _():
        o_ref[kept:, :] = new_ref[...]

    @pl.when(jnp.logical_not(last))
    def _():
        o_ref[kept:, :] = next_ref[...]


def _roll_cache(cache, new_rows):
    depth, nb, length = cache.shape[:3]
    tokens = min(length, ROLL_TOKENS)
    rows = tokens * KV_ROWS
    flat = cache.reshape(depth, nb, length * KV_ROWS, HEAD_DIM)
    token = (None, None, KV_ROWS, HEAD_DIM)
    out = pl.pallas_call(
        _roll_kernel,
        grid=(depth, nb, length // tokens),
        in_specs=[pl.BlockSpec((None, None, rows, HEAD_DIM), lambda d, b, c: (d, b, c, 0)),
                  pl.BlockSpec(token, lambda d, b, c: (d, b, jnp.minimum((c + 1) * tokens, length - 1), 0)),
                  pl.BlockSpec(token, lambda d, b, c: (d, b, 0, 0))],
        out_specs=pl.BlockSpec((None, None, rows, HEAD_DIM), lambda d, b, c: (d, b, c, 0)),
        out_shape=jax.ShapeDtypeStruct(flat.shape, cache.dtype),
        compiler_params=_params(("parallel", "parallel", "arbitrary"), 2 * _nbytes((rows, HEAD_DIM), F32)),
        name="roll_cache",
    )(flat, flat, new_rows.reshape(depth, nb, KV_ROWS, HEAD_DIM))
    return out.reshape(cache.shape)


def _t5_bucket(dist):
    dist = np.asarray(dist)
    max_exact = NUM_BUCKETS // 2
    large = max_exact + (np.log(np.maximum(dist, 1) / max_exact) / np.log(REL_MAX_DIST / max_exact)
                         * (NUM_BUCKETS - max_exact)).astype(np.int32)
    large = np.minimum(large, NUM_BUCKETS - 1)
    return np.where(dist < max_exact, dist, large).astype(np.int32)


def _bias_tables(rel_bias):
    hpg = HEADS_PER_GROUP
    prompt, dec, dec0 = [], [], []
    for gi, (_, dil) in enumerate(DILATION_GROUPS):
        heads = rel_bias[:, gi * hpg:(gi + 1) * hpg].astype(F32)
        by_off = jnp.take(heads, _t5_bucket(np.arange(BLK + 1) * dil), axis=0).T
        w = jnp.concatenate([by_off[:, ::-1], jnp.full((hpg, BLK), NEG_INF, F32)], axis=1)
        tab = jnp.tile(w, (1, BLK))[:, :BLK * 2 * BLK].reshape(hpg, BLK, 2 * BLK)
        prompt.append(tab)
        dec.append(jnp.broadcast_to(by_off[:, :0:-1].T[:, :, None], (BLK, hpg, HEAD_DIM)))
        dec0.append(jnp.broadcast_to(by_off[:, 0:1], (hpg, HEAD_DIM)))
    return jnp.stack(prompt), jnp.stack(dec), jnp.stack(dec0)


def _prepare_weights(w_in, w_attn_br, w_sgu_br, w_out, w_ffn_gate, w_ffn_up, w_ffn_down,
                     w_ple_gate, w_ple_proj):
    return dict(
        w_in=w_in, w_gate=w_ffn_gate, w_up=w_ffn_up,
        w_attn_br=w_attn_br.astype(BF16), w_sgu_br=w_sgu_br.astype(BF16), w_out=w_out.astype(BF16),
        w_down=w_ffn_down.astype(BF16),
        w_ple_gate=w_ple_gate.astype(BF16), w_ple_proj=w_ple_proj.astype(BF16))


DOWN_TK = 2816


def _layer(prompt, sample, pp, ps, caches, wts, layer, depth, kv_prev, tables, g_mix_next, g_q, g_k,
           g_sgu, w_sgu, b_sgu, g_ffn, g_ple, batch, seq):
    (xp, h, inv), (xs, hs, invs) = prompt, sample
    tm, nb = 1024, xs.shape[0]
    bias_tab, bias_s, bias0_s = tables
    w_in = wts["w_in"]

    q_hm, zq = _q_proj(h, inv, hs, invs, w_in, layer, g_q, batch, seq, RES_TM)
    uv, zu = _proj_act(h, inv, hs, invs, w_in, layer, COL_U * TN // PTN, 2 * GM_WIDTH // PTN, "gelu",
                       RES_TM)
    gates, zg = _proj_act(h, inv, hs, invs, w_in, layer, COL_GATE * TN // PTN, 2 * D_MODEL // PTN,
                          "sigmoid", RES_TM)
    kv_hm, kvs, zkv = [], [], []
    for gi in range(N_GROUPS):
        hm, kv, zkv_g = _kv_proj(h, inv, hs, invs, w_in, layer, depth, g_k, gi, kv_prev[gi], batch, seq,
                                 RES_TM)
        kv_hm.append(hm)
        kvs.append(kv)
        zkv.append(zkv_g)

    attn_p = _attention(q_hm, kv_hm, bias_tab, batch, seq).reshape(batch * seq, ATTN_OUT)
    sgu_p = _sgu(uv, g_sgu, w_sgu, b_sgu, 512)
    z = jnp.concatenate([zq] + [zg_[:, :TN] for zg_ in zkv] + [zg_[:, TN:] for zg_ in zkv] + [zu, zg],
                        axis=1)
    kvn, attn_s, sguv, sgu_s, ga_s, gb_s = _sample_mix(z, caches, layer, g_q, g_k, g_sgu, w_sgu, b_sgu,
                                                       bias_s, bias0_s)

    wa, wb = wts["w_attn_br"], wts["w_sgu_br"]
    merged_p = _merge(attn_p, sgu_p, wa, wb, layer, gates, gates, D_MODEL // TN, BF16, tm)
    merged_s = _merge(attn_s.reshape(nb, ATTN_OUT), sgu_s.reshape(nb, GM_WIDTH), wa, wb, layer,
                      ga_s.reshape(nb, D_MODEL), gb_s.reshape(nb, D_MODEL), 0, F32, nb)
    xp, h, inv = _mm_res(merged_p, wts["w_out"], layer, xp, tm, TN // 2, D_MODEL, g_ffn, BF16)
    xs, hs, invs = _mm_res(merged_s, wts["w_out"], layer, xs, nb, TN, D_MODEL, g_ffn, F32)

    ff_p, ff_s = _glu(h, inv, hs, invs, wts["w_gate"], wts["w_up"], layer, tm)
    xp, h, inv = _mm_res(ff_p, wts["w_down"], layer, xp, tm, TN, DOWN_TK, g_ple, BF16)
    xs, hs, invs = _mm_res(ff_s, wts["w_down"], layer, xs, nb, TN, DOWN_TK, g_ple, F32)

    wg, wp = wts["w_ple_gate"], wts["w_ple_proj"]
    if g_mix_next is None:
        prompt = (_ple(h, inv, pp, wg, wp, layer, xp, tm), None, None)
        sample = (_ple(hs, invs, ps, wg, wp, layer, xs, nb), None, None)
    else:
        prompt = _ple(h, inv, pp, wg, wp, layer, xp, tm, g_mix_next, BF16)
        sample = _ple(hs, invs, ps, wg, wp, layer, xs, nb, g_mix_next, F32)
    return prompt, sample, kvs, kvn, sguv


def kernel(x_prompt, x_sample, cache_kv_w128, cache_kv_w512, cache_kv_w2048, p_prompt, p_sample,
           rel_bias, g_mix, w_in, g_q, g_k, g_sgu, w_sgu, b_sgu, w_attn_br, w_sgu_br, w_out, g_ffn,
           w_ffn_gate, w_ffn_up, w_ffn_down, g_ple, w_ple_gate, w_ple_proj):
    batch, seq, _ = x_prompt.shape
    nb = x_sample.shape[0]
    depth = w_in.shape[0]
    caches = (cache_kv_w128, cache_kv_w512, cache_kv_w2048)
    tables = _bias_tables(rel_bias)
    wts = _prepare_weights(w_in, w_attn_br, w_sgu_br, w_out, w_ffn_gate, w_ffn_up, w_ffn_down,
                           w_ple_gate, w_ple_proj)

    xp = x_prompt.reshape(batch * seq, D_MODEL)
    xs = x_sample.reshape(nb, D_MODEL)
    pp = p_prompt.reshape(depth, batch * seq, PLE_DIM)
    ps = p_sample.reshape(depth, nb, PLE_DIM)
    prompt = (xp, _rmsnorm(xp, g_mix[0], BF16, 256), jnp.ones((batch * seq, 1), F32))
    sample = (xs, _rmsnorm(xs, g_mix[0], F32, nb), jnp.ones((nb, 1), F32))
    kv_p = [None] * N_GROUPS
    kv_new = [[] for _ in range(N_GROUPS)]
    sgu_v = []
    for i in range(depth):
        g_mix_next = g_mix[i + 1] if i + 1 < depth else None
        params = (g_mix_next, g_q[i], g_k[i], g_sgu[i], w_sgu[i], b_sgu[i], g_ffn[i], g_ple[i])
        prompt, sample, kv_p, kvn, sguv = _layer(prompt, sample, pp, ps, caches, wts, i, depth, kv_p,
                                                 tables, *params, batch, seq)
        for gi in range(N_GROUPS):
            kv_new[gi].append(kvn[:, gi])
        sgu_v.append(sguv)

    xp, xs = prompt[0], sample[0]
    kv_p = [kv.reshape(depth, batch, -1, 2, HEADS_PER_GROUP, HEAD_DIM) for kv in kv_p]
    kv_s = [_roll_cache(c, jnp.stack(kv_new[gi], axis=0)) for gi, c in enumerate(caches)]
    return (xp.reshape(batch, seq, D_MODEL), xs.reshape(nb, 1, D_MODEL), *kv_p, *kv_s,
            jnp.stack(sgu_v, axis=0))
```
